```python
import jax, jax.numpy as jnp
from jax import lax
import numpy as np

D_MODEL = 1024
BATCH = 16
SEQ = 2048
DEPTH = 2
DEC_BATCH = 32
DEC_SEQ = 64
PAST_LEN = 4096

CHUNK = 64
GMLP_CHUNK = 128
GMLP_GROUPS = 4
GMLP_WIDTH = D_MODEL
GMLP_GROUP_DIM = GMLP_WIDTH // GMLP_GROUPS
HGRN_EXPAND = 128
HGRN_HEADS = D_MODEL // HGRN_EXPAND
HGRN_DK = HGRN_EXPAND
HGRN_DV = D_MODEL // HGRN_HEADS
HGRN_QK_WIDTH = HGRN_HEADS * HGRN_DK
HGRN_WIDTH = HGRN_HEADS * HGRN_DV
HGRN_BLOCK = 16
N_BRANCH = 2
IN_SIZES = (GMLP_WIDTH, GMLP_WIDTH, GMLP_WIDTH, HGRN_QK_WIDTH, HGRN_QK_WIDTH, HGRN_WIDTH, HGRN_WIDTH, D_MODEL, D_MODEL)
IN_COLS = sum(IN_SIZES)
EPS = 1e-6

kernel_name = "gmlp_hgrn2_gated_streaming_step"


def rms_norm(x, g):
    xf = x.astype(jnp.float32)
    r = lax.rsqrt(jnp.mean(xf * xf, axis=-1, keepdims=True) + EPS)
    return (xf * r).astype(x.dtype) * g


def layer_norm(x, g, b):
    xf = x.astype(jnp.float32)
    mu = jnp.mean(xf, axis=-1, keepdims=True)
    var = jnp.mean(jnp.square(xf - mu), axis=-1, keepdims=True)
    return ((xf - mu) * lax.rsqrt(var + EPS)).astype(x.dtype) * g + b


def gmlp_spatial(u, v, w_s, b_s):
    B, T, _ = v.shape
    n = -(-T // GMLP_CHUNK)
    pad = n * GMLP_CHUNK - T
    vp = jnp.pad(v, ((0, 0), (0, pad), (0, 0))).reshape(B, n, GMLP_CHUNK, GMLP_GROUPS, GMLP_GROUP_DIM)
    mask = jnp.tril(jnp.ones((GMLP_CHUNK, GMLP_CHUNK), dtype=bool))
    w = jnp.where(mask[None], w_s, jnp.zeros_like(w_s))
    s = jnp.einsum('gts,bnsgc->bntgc', w, vp) + b_s.T[None, None, :, :, None]
    s = s.reshape(B, n * GMLP_CHUNK, GMLP_WIDTH)[:, :T]
    return u * s


def hgrn2_recurrence(q, k, i, log_f, s0):
    B, T = q.shape[:2]
    n = -(-T // HGRN_BLOCK)
    pad = n * HGRN_BLOCK - T

    def blocks(a):
        a = jnp.pad(a, ((0, 0), (0, pad), (0, 0), (0, 0)))
        return a.reshape(B, n, HGRN_BLOCK, a.shape[2], a.shape[3]).transpose(1, 0, 3, 2, 4)

    qb, kb, ib, fb = blocks(q), blocks(k), blocks(i), blocks(log_f)
    mask = jnp.tril(jnp.ones((HGRN_BLOCK, HGRN_BLOCK), dtype=bool))[None, None, :, :, None]

    def step(S, blk):
        qc, kc, ic, fc = blk
        b = jnp.cumsum(fc, axis=2)
        b_last = b[:, :, -1:, :]
        o_inter = jnp.einsum('bhtk,bhkv->bhtv', qc * jnp.exp(b), S)
        rel = jnp.where(mask, b[:, :, :, None, :] - b[:, :, None, :, :], -jnp.inf)
        a = jnp.einsum('bhtk,bhsk,bhtsk->bhts', qc, kc, jnp.exp(rel))
        o_intra = jnp.einsum('bhts,bhsv->bhtv', a, ic)
        S_new = jnp.exp(b_last[:, :, 0, :])[..., None] * S + jnp.einsum('bhsk,bhsv->bhkv', kc * jnp.exp(b_last - b), ic)
        return S_new, o_inter + o_intra

    S, o = lax.scan(step, s0, (qb, kb, ib, fb))
    o = o.transpose(1, 0, 3, 2, 4).reshape(B, n * HGRN_BLOCK, HGRN_HEADS, HGRN_DV)[:, :T]
    return o, S


def trunk_layer(x, c, s0, lb, w_ada, b_ada, norm_g, w_in, ln_v_g, ln_v_b, w_s, b_s, gnorm_g, w_pa, w_pb, w_o):
    B, T, _ = x.shape
    mod = (jax.nn.silu(c) @ w_ada + b_ada)[:, None, :]
    shift, scale, gate = jnp.split(mod, 3, axis=-1)
    h = rms_norm(x, norm_g) * (1 + scale) + shift
    proj = h @ w_in
    edges, acc = [], 0
    for s in IN_SIZES[:-1]:
        acc += s
        edges.append(acc)
    u, v, z_a, q, f, i, z_b, g_a, g_b = jnp.split(proj, edges, axis=-1)

    u = jax.nn.gelu(u)
    v = layer_norm(jax.nn.gelu(v), ln_v_g, ln_v_b)
    y_a = gmlp_spatial(u, v, w_s, b_s) * jax.nn.silu(z_a)

    qh = jax.nn.silu(q).astype(jnp.float32).reshape(B, T, HGRN_HEADS, HGRN_DK)
    zf = f.astype(jnp.float32).reshape(B, T, HGRN_HEADS, HGRN_DK)
    lbh = lb.reshape(HGRN_HEADS, HGRN_DK)
    log_f = jnp.logaddexp(jnp.log(lbh), jnp.log1p(-lbh) + jax.nn.log_sigmoid(zf))
    k = (1.0 - lbh) * jax.nn.sigmoid(-zf)
    ih = i.astype(jnp.float32).reshape(B, T, HGRN_HEADS, HGRN_DV)
    o, s_new = hgrn2_recurrence(qh, k, ih, log_f, s0.astype(jnp.float32))
    o = rms_norm(o.astype(x.dtype), gnorm_g.reshape(HGRN_HEADS, HGRN_DV)).reshape(B, T, HGRN_WIDTH)
    y_b = o * jax.nn.silu(z_b)

    m = jax.nn.sigmoid(g_a) * (y_a @ w_pa) + jax.nn.sigmoid(g_b) * (y_b @ w_pb)
    x = x + gate * (m @ w_o)
    return x, s_new.astype(x.dtype), v


def setup_inputs(seed: int = 0) -> dict:
    key = jax.random.key(seed)
    ks = jax.random.split(key, 24)
    nrm = jax.random.normal
    D = D_MODEL
    return {
        "x_prompt": nrm(ks[0], (BATCH, SEQ, D), jnp.float32),
        "x_sample": nrm(ks[1], (DEC_BATCH, DEC_SEQ, D), jnp.float32),
        "state_hgrn": 0.5 * nrm(ks[2], (DEPTH, DEC_BATCH, HGRN_HEADS, HGRN_DK, HGRN_DV), jnp.float32),
        "c_prompt": nrm(ks[3], (BATCH, D), jnp.float32),
        "c_sample": nrm(ks[4], (DEC_BATCH, D), jnp.float32),
        "w_ada": 0.5 * D ** -0.5 * nrm(ks[5], (DEPTH, D, 3 * D), jnp.float32),
        "b_ada": 0.01 * nrm(ks[6], (DEPTH, 3 * D), jnp.float32),
        "norm_g": 1.0 + 0.05 * nrm(ks[7], (DEPTH, D), jnp.float32),
        "w_in": D ** -0.5 * nrm(ks[8], (DEPTH, D, IN_COLS), jnp.float32),
        "ln_v_g": 1.0 + 0.05 * nrm(ks[9], (DEPTH, GMLP_WIDTH), jnp.float32),
        "ln_v_b": 0.01 * nrm(ks[10], (DEPTH, GMLP_WIDTH), jnp.float32),
        "w_s": 0.5 * GMLP_CHUNK ** -0.5 * nrm(ks[11], (DEPTH, GMLP_GROUPS, GMLP_CHUNK, GMLP_CHUNK), jnp.float32),
        "b_s": 1.0 + 0.1 * nrm(ks[12], (DEPTH, GMLP_GROUPS, GMLP_CHUNK), jnp.float32),
        "lb_raw": nrm(ks[13], (DEPTH, HGRN_QK_WIDTH), jnp.float32),
        "gnorm_g": 1.0 + 0.05 * nrm(ks[14], (DEPTH, HGRN_WIDTH), jnp.float32),
        "w_pa": GMLP_WIDTH ** -0.5 * nrm(ks[15], (DEPTH, GMLP_WIDTH, D), jnp.float32),
        "w_pb": HGRN_WIDTH ** -0.5 * nrm(ks[16], (DEPTH, HGRN_WIDTH, D), jnp.float32),
        "w_o": D ** -0.5 * nrm(ks[17], (DEPTH, D, D), jnp.float32),
        "final_g": 1.0 + 0.05 * nrm(ks[18], (D,), jnp.float32),
    }


def reference(x_prompt, x_sample, state_hgrn, c_prompt, c_sample, w_ada, b_ada, norm_g, w_in, ln_v_g, ln_v_b,
              w_s, b_s, lb_raw, gnorm_g, w_pa, w_pb, w_o, final_g):
    lbs = jnp.cumsum(jax.nn.softmax(lb_raw.astype(jnp.float32), axis=0), axis=0)
    lbs = lbs - lbs[0:1]
    xp, xs = x_prompt, x_sample
    s_zero = jnp.zeros((x_prompt.shape[0], HGRN_HEADS, HGRN_DK, HGRN_DV), jnp.float32)
    sp_list, ss_list, vs_list = [], [], []
    for l in range(DEPTH):
        wl = (w_ada[l], b_ada[l], norm_g[l], w_in[l], ln_v_g[l], ln_v_b[l], w_s[l], b_s[l], gnorm_g[l],
              w_pa[l], w_pb[l], w_o[l])
        xp, sp, _ = trunk_layer(xp, c_prompt, s_zero, lbs[l], *wl)
        xs, ss, vs = trunk_layer(xs, c_sample, state_hgrn[l], lbs[l], *wl)
        sp_list.append(sp)
        ss_list.append(ss)
        vs_list.append(vs)
    y_prompt = rms_norm(xp, final_g)
    y_sample = rms_norm(xs, final_g)
    new_state_hgrn_prompt = jnp.stack(sp_list)
    new_state_hgrn_sample = jnp.stack(ss_list)
    new_v_gmlp_sample = jnp.stack(vs_list)
    return (y_prompt, y_sample, new_state_hgrn_prompt, new_state_hgrn_sample, new_v_gmlp_sample)
```

```python
import functools
import math
from typing import NamedTuple

import jax
import jax.numpy as jnp
from jax import lax
from jax.experimental import pallas as pl
from jax.experimental.pallas import tpu as pltpu

F32 = jnp.float32
BF16 = jnp.bfloat16

EPS = 1e-6
GMLP_CHUNK = 128
GMLP_GROUPS = 4
HGRN_HEADS = 8
HGRN_DK = 128
HGRN_DV = 128
N_IN_SLABS = 9
SLAB_U, SLAB_V, SLAB_ZA, SLAB_Q, SLAB_F, SLAB_I, SLAB_ZB, SLAB_GA, SLAB_GB = range(N_IN_SLABS)

HGRN_CHUNK = 128
HGRN_DIAG = 16
EXP_CLAMP = 80.0
TILE_ROWS = 256
V7X_VMEM_LIMIT_BYTES = 60000 * 1024

_NT = (((1,), (1,)), ((), ()))
_TN = (((0,), (0,)), ((), ()))


class _Cfg(NamedTuple):
    layer: int
    depth: int
    nb: int
    tt: int
    gc: int
    hc: int
    has_s0: bool
    emit_v: bool
    final: bool


def _gelu_tanh(x):
    c = math.sqrt(2.0 / math.pi)
    return 0.5 * x * (1.0 + jnp.tanh(c * (x + 0.044715 * (x * x * x))))


def _sigmoid(x):
    return 1.0 / (1.0 + jnp.exp(-x))


def _silu(x):
    return x * _sigmoid(x)


def _rms(x):
    return lax.rsqrt(jnp.mean(x * x, axis=-1, keepdims=True) + EPS)


def _rows_from(b, idx, n_rep):
    if idx < 0:
        return jnp.zeros((n_rep, b.shape[1]), b.dtype)
    return jnp.broadcast_to(b[idx:idx + 1, :], (n_rep, b.shape[1]))


def _forget_lower_bound(lbraw_ref, layer, depth):
    raw = lbraw_ref[...]
    mx = jnp.max(raw, axis=0, keepdims=True)
    ex = jnp.exp(raw - mx)
    den = jnp.sum(ex, axis=0, keepdims=True)
    num = jnp.zeros_like(den)
    for j in range(1, layer + 1):
        num = num + ex[j:j + 1, :]
    return num / den


def _hgrn_chunk(cfg, qs, kk, lf, iv, gate_b, gng, st_ref, n, yb_ref, row0):
    c, d = qs.shape
    rowi = lax.broadcasted_iota(jnp.int32, (c, d), 0)

    b = lf
    sh = 1
    while sh < c:
        b = b + jnp.where(rowi >= sh, pltpu.roll(b, sh, 0), 0.0)
        sh *= 2
    b_last = b[c - 1:c, :]
    q_in = (qs * jnp.exp(b)).astype(BF16)
    k_end = (kk * jnp.exp(b_last - b)).astype(BF16)
    s_decay = jnp.exp(b_last)

    ti = lax.broadcasted_iota(jnp.int32, (c, c), 0)
    si = lax.broadcasted_iota(jnp.int32, (c, c), 1)
    levels = []
    half = c // 2
    while half >= HGRN_DIAG:
        blk = 2 * half
        m = jnp.concatenate([_rows_from(b, k * blk + half - 1, blk) for k in range(c // blk)], axis=0)
        decay = jnp.exp(-jnp.abs(b - m))
        second = (rowi & (blk - 1)) >= half
        x = (jnp.where(second, qs, kk) * decay).astype(BF16)
        mask = ((ti ^ si) < blk) & ((ti & half) != 0) & ((si & half) == 0)
        levels.append((x, mask))
        half //= 2
    r = jnp.concatenate([_rows_from(b, k * HGRN_DIAG - 1, HGRN_DIAG) for k in range(c // HGRN_DIAG)], axis=0)
    q_dg = (qs * jnp.exp(b - r)).astype(BF16)
    k_dg = (kk * jnp.exp(jnp.minimum(r - b, EXP_CLAMP))).astype(BF16)
    mask_dg = ((ti ^ si) < HGRN_DIAG) & (si <= ti)

    for h in range(HGRN_HEADS):
        hs = slice(h * HGRN_DK, (h + 1) * HGRN_DK)
        vs = slice(h * HGRN_DV, (h + 1) * HGRN_DV)
        a = jnp.where(mask_dg,
                      lax.dot_general(q_dg[:, hs], k_dg[:, hs], _NT, preferred_element_type=F32), 0.0)
        for x, mask in levels:
            xh = x[:, hs]
            a = jnp.where(mask, lax.dot_general(xh, xh, _NT, preferred_element_type=F32), a)
        st = st_ref[n, h]
        o = (lax.dot_general(q_in[:, hs], st.astype(BF16), _NT, preferred_element_type=F32)
             + jnp.dot(a.astype(BF16), iv[:, vs], preferred_element_type=F32))
        st_ref[n, h] = (st * s_decay[:, hs]
                        + lax.dot_general(iv[:, vs], k_end[:, hs], _TN, preferred_element_type=F32))
        on = o * _rms(o) * gng[:, vs]
        yb_ref[row0:row0 + c, vs] = (on * gate_b[:, vs]).astype(BF16)


def _layer_body(cfg, *refs):
    refs = list(refs)
    x_ref, mod_ref = refs[:2]
    pos = 2
    s0_ref = None
    if cfg.has_s0:
        s0_ref = refs[pos]
        pos += 1
    (normg_ref, win_ref, lnvg_ref, lnvb_ref, ws_ref, bs_ref, lbraw_ref, gng_ref,
     wpa_ref, wpb_ref, wo_ref, fing_ref) = refs[pos:pos + 12]
    pos += 12
    y_ref, sout_ref = refs[pos:pos + 2]
    pos += 2
    v_ref = None
    if cfg.emit_v:
        v_ref = refs[pos]
        pos += 1
    st_ref, yb_ref = refs[pos:pos + 2]

    nb, tt = cfg.nb, cfg.tt
    rows = nb * tt
    d = x_ref.shape[-1]
    t_idx = pl.program_id(1)
    n_t = pl.num_programs(1)

    hs = []
    for n in range(nb):
        x = x_ref[n]
        shift = mod_ref[n, 0:1, :]
        scale = mod_ref[n, 1:2, :]
        hs.append(((x * _rms(x)) * normg_ref[...] * (1.0 + scale) + shift).astype(BF16))
    h = hs[0] if nb == 1 else jnp.concatenate(hs, axis=0)

    def proj(slab):
        return jnp.dot(h, win_ref[:, slab * d:(slab + 1) * d], preferred_element_type=F32)

    v = _gelu_tanh(proj(SLAB_V))
    mu = jnp.mean(v, axis=-1, keepdims=True)
    vc = v - mu
    var = jnp.mean(vc * vc, axis=-1, keepdims=True)
    vn = vc * lax.rsqrt(var + EPS) * lnvg_ref[...] + lnvb_ref[...]
    if cfg.emit_v:
        for n in range(nb):
            v_ref[n] = vn[n * tt:(n + 1) * tt]
    vb = vn.astype(BF16)
    gc = cfg.gc
    gdim = d // GMLP_GROUPS
    tril = (lax.broadcasted_iota(jnp.int32, (gc, gc), 0) >= lax.broadcasted_iota(jnp.int32, (gc, gc), 1))
    w_tril = [jnp.where(tril, ws_ref[g], 0.0).astype(BF16) for g in range(GMLP_GROUPS)]
    s_chunks = []
    for j in range(rows // gc):
        parts = [jnp.dot(w_tril[g], vb[j * gc:(j + 1) * gc, g * gdim:(g + 1) * gdim],
                         preferred_element_type=F32) + bs_ref[:, g:g + 1]
                 for g in range(GMLP_GROUPS)]
        s_chunks.append(jnp.concatenate(parts, axis=1))
    s = s_chunks[0] if len(s_chunks) == 1 else jnp.concatenate(s_chunks, axis=0)
    ya = (_gelu_tanh(proj(SLAB_U)) * s * _silu(proj(SLAB_ZA))).astype(BF16)
    m_a = _sigmoid(proj(SLAB_GA)) * jnp.dot(ya, wpa_ref[...], preferred_element_type=F32)

    qs = _silu(proj(SLAB_Q))
    z = proj(SLAB_F)
    e = jnp.exp(-jnp.abs(z))
    log_sig = jnp.minimum(z, 0.0) - jnp.log1p(e)
    sig_neg = jnp.where(z >= 0.0, e, 1.0) / (1.0 + e)
    if cfg.layer == 0:
        lf = log_sig
        kk = sig_neg
    else:
        lb = _forget_lower_bound(lbraw_ref, cfg.layer, cfg.depth)
        ya_ = jnp.log1p(-lb) + log_sig
        la = jnp.log(lb)
        lf = jnp.maximum(la, ya_) + jnp.log1p(jnp.exp(-jnp.abs(la - ya_)))
        kk = (1.0 - lb) * sig_neg
    iv = proj(SLAB_I).astype(BF16)
    gate_b = _silu(proj(SLAB_ZB))
    gng = gng_ref[...]

    hc = cfg.hc
    cps = tt // hc
    for j in range(rows // hc):
        n = j // cps
        if j % cps == 0:
            @pl.when(t_idx == 0)
            def _():
                for hd in range(HGRN_HEADS):
                    if cfg.has_s0:
                        st_ref[n, hd] = s0_ref[n, hd].T
                    else:
                        st_ref[n, hd] = jnp.zeros((HGRN_DV, HGRN_DK), F32)
        sl = slice(j * hc, (j + 1) * hc)
        _hgrn_chunk(cfg, qs[sl], kk[sl], lf[sl], iv[sl], gate_b[sl], gng, st_ref, n, yb_ref, j * hc)
        if j % cps == cps - 1:
            @pl.when(t_idx == n_t - 1)
            def _():
                for hd in range(HGRN_HEADS):
                    sout_ref[n, hd] = st_ref[n, hd].T

    m_b = _sigmoid(proj(SLAB_GB)) * jnp.dot(yb_ref[...], wpb_ref[...], preferred_element_type=F32)
    upd = jnp.dot((m_a + m_b).astype(BF16), wo_ref[...], preferred_element_type=F32)
    for n in range(nb):
        gate = mod_ref[n, 2:3, :]
        xo = x_ref[n] + gate * upd[n * tt:(n + 1) * tt]
        if cfg.final:
            xo = (xo * _rms(xo)) * fing_ref[...]
        y_ref[n] = xo


def _tiling(batch, seq):
    chunk = min(seq, GMLP_CHUNK)
    assert seq % chunk == 0 and chunk % HGRN_DIAG == 0
    if seq >= TILE_ROWS:
        assert seq % TILE_ROWS == 0
        return 1, TILE_ROWS, chunk
    nb = max(1, TILE_ROWS // seq)
    while batch % nb:
        nb -= 1
    return nb, seq, chunk


def _trunk_layer(layer, depth, x, mod, s0, w, final, emit_v):
    batch, seq, d = x.shape
    nb, tt, chunk = _tiling(batch, seq)
    hc = min(chunk, HGRN_CHUNK)
    cfg = _Cfg(layer=layer, depth=depth, nb=nb, tt=tt, gc=chunk, hc=hc,
               has_s0=s0 is not None, emit_v=emit_v, final=final)
    rows = nb * tt

    def full(a):
        nd = a.ndim
        return pl.BlockSpec(a.shape, lambda b, t, _nd=nd: (0,) * _nd)

    tok_spec = pl.BlockSpec((nb, tt, d), lambda b, t: (b, t, 0))
    st_spec = pl.BlockSpec((nb, HGRN_HEADS, HGRN_DK, HGRN_DV), lambda b, t: (b, 0, 0, 0))
    ws = w["w_s"][:, :chunk, :chunk]
    bs_t = w["b_s"][:, :chunk].T
    params = [w["norm_g"], w["w_in"], w["ln_v_g"], w["ln_v_b"], ws, bs_t, w["lb_raw"], w["gnorm_g"],
              w["w_pa"], w["w_pb"], w["w_o"], w["final_g"]]
    args = [x, mod] + ([s0] if s0 is not None else []) + params
    in_specs = ([tok_spec, pl.BlockSpec((nb, 3, d), lambda b, t: (b, 0, 0))]
                + ([st_spec] if s0 is not None else []) + [full(a) for a in params])
    out_shape = [jax.ShapeDtypeStruct(x.shape, F32),
                 jax.ShapeDtypeStruct((batch, HGRN_HEADS, HGRN_DK, HGRN_DV), F32)]
    out_specs = [tok_spec, st_spec]
    if emit_v:
        out_shape.append(jax.ShapeDtypeStruct(x.shape, F32))
        out_specs.append(tok_spec)
    return pl.pallas_call(
        functools.partial(_layer_body, cfg),
        grid=(batch // nb, seq // tt),
        in_specs=in_specs,
        out_specs=out_specs,
        out_shape=out_shape,
        scratch_shapes=[pltpu.VMEM((nb, HGRN_HEADS, HGRN_DV, HGRN_DK), F32),
                        pltpu.VMEM((rows, d), BF16)],
        compiler_params=pltpu.CompilerParams(
            dimension_semantics=("arbitrary", "arbitrary"),
            vmem_limit_bytes=V7X_VMEM_LIMIT_BYTES),
        name=f"trunk_layer{layer}_{'sample' if emit_v else 'prompt'}",
    )(*args)


def _ada_body(c_ref, w_ref, b_ref, o_ref):
    c = c_ref[...]
    o_ref[0] = jnp.dot(_silu(c), w_ref[0], preferred_element_type=F32,
                       precision=lax.Precision.HIGHEST) + b_ref[0]


def _ada_modulation(c_all, w_ada, b_ada):
    depth, d, d3 = w_ada.shape
    n = c_all.shape[0]
    return pl.pallas_call(
        _ada_body,
        grid=(depth, d3 // d),
        in_specs=[pl.BlockSpec((n, d), lambda l, j: (0, 0)),
                  pl.BlockSpec((1, d, d), lambda l, j: (l, 0, j)),
                  pl.BlockSpec((1, 1, d), lambda l, j: (l, 0, j))],
        out_specs=pl.BlockSpec((1, n, d), lambda l, j: (l, 0, j)),
        out_shape=jax.ShapeDtypeStruct((depth, n, d3), F32),
        compiler_params=pltpu.CompilerParams(dimension_semantics=("arbitrary", "arbitrary")),
        name="ada_modulation",
    )(c_all, w_ada, b_ada.reshape(depth, 1, d3))


def kernel(x_prompt, x_sample, state_hgrn, c_prompt, c_sample, w_ada, b_ada, norm_g, w_in, ln_v_g, ln_v_b,
           w_s, b_s, lb_raw, gnorm_g, w_pa, w_pb, w_o, final_g):
    depth, d = norm_g.shape
    n_prompt = x_prompt.shape[0]
    mod = _ada_modulation(jnp.concatenate([c_prompt, c_sample], axis=0), w_ada, b_ada)
    mod = mod.reshape(depth, -1, 3, d)

    xp, xs = x_prompt, x_sample
    sp_list, ss_list, vs_list = [], [], []
    for l in range(depth):
        w = dict(norm_g=norm_g[l][None], w_in=w_in[l].astype(BF16), ln_v_g=ln_v_g[l][None],
                 ln_v_b=ln_v_b[l][None], w_s=w_s[l], b_s=b_s[l], lb_raw=lb_raw, gnorm_g=gnorm_g[l][None],
                 w_pa=w_pa[l].astype(BF16), w_pb=w_pb[l].astype(BF16), w_o=w_o[l].astype(BF16),
                 final_g=final_g[None])
        final = l == depth - 1
        xp, sp = _trunk_layer(l, depth, xp, mod[l, :n_prompt], None, w, final, False)
        xs, ss, vs = _trunk_layer(l, depth, xs, mod[l, n_prompt:], state_hgrn[l], w, final, True)
        sp_list.append(sp)
        ss_list.append(ss)
        vs_list.append(vs)
    return (xp, xs, jnp.stack(sp_list), jnp.stack(ss_list), jnp.stack(vs_list))
```

```python
import functools
import math
from typing import NamedTuple

import jax
import jax.numpy as jnp
from jax import lax
from jax.experimental import pallas as pl
from jax.experimental.pallas import tpu as pltpu

F32 = jnp.float32
BF16 = jnp.bfloat16

EPS = 1e-6
GMLP_CHUNK = 128
GMLP_GROUPS = 4
HGRN_HEADS = 8
HGRN_DK = 128
HGRN_DV = 128
N_IN_SLABS = 9
SLAB_U, SLAB_V, SLAB_ZA, SLAB_Q, SLAB_F, SLAB_I, SLAB_ZB, SLAB_GA, SLAB_GB = range(N_IN_SLABS)

HGRN_CHUNK = 128
HGRN_DIAG = 16
EXP_CLAMP = 80.0
TILE_ROWS = 256
V7X_VMEM_LIMIT_BYTES = 60000 * 1024

_NT = (((1,), (1,)), ((), ()))
_TN = (((0,), (0,)), ((), ()))


class _Cfg(NamedTuple):
    layer: int
    depth: int
    nb: int
    tt: int
    gc: int
    hc: int
    has_s0: bool
    emit_v: bool
    final: bool


def _gelu_tanh(x):
    c = math.sqrt(2.0 / math.pi)
    return 0.5 * x * (1.0 + jnp.tanh(c * (x + 0.044715 * (x * x * x))))


def _sigmoid(x):
    return 1.0 / (1.0 + jnp.exp(-x))


def _silu(x):
    return x * _sigmoid(x)


def _rms(x):
    return lax.rsqrt(jnp.mean(x * x, axis=-1, keepdims=True) + EPS)


def _pack_rows_bf16(w):
    k, n = w.shape
    wb = w.astype(BF16).reshape(k // 2, 2, n)
    return lax.bitcast_convert_type(jnp.swapaxes(wb, -1, -2), jnp.uint32)


def _as_bf16(packed):
    return pltpu.bitcast(packed, BF16)


def _rows_from(b, idx, n_rep):
    if idx < 0:
        return jnp.zeros((n_rep, b.shape[1]), b.dtype)
    return jnp.broadcast_to(b[idx:idx + 1, :], (n_rep, b.shape[1]))


def _forget_lower_bound(lbraw_ref, layer, depth):
    raw = lbraw_ref[...]
    mx = jnp.max(raw, axis=0, keepdims=True)
    ex = jnp.exp(raw - mx)
    den = jnp.sum(ex, axis=0, keepdims=True)
    num = jnp.zeros_like(den)
    for j in range(1, layer + 1):
        num = num + ex[j:j + 1, :]
    return num / den


def _hgrn_prepare(qs, kk, lf):
    c, d = qs.shape
    rowi = lax.broadcasted_iota(jnp.int32, (c, d), 0)

    b = lf
    sh = 1
    while sh < c:
        b = b + jnp.where(rowi >= sh, pltpu.roll(b, sh, 0), 0.0)
        sh *= 2
    b_last = b[c - 1:c, :]
    q_in = (qs * jnp.exp(b)).astype(BF16)
    k_end = (kk * jnp.exp(b_last - b)).astype(BF16)
    s_decay = jnp.exp(b_last)

    ti = lax.broadcasted_iota(jnp.int32, (c, c), 0)
    si = lax.broadcasted_iota(jnp.int32, (c, c), 1)
    levels = []
    half = c // 2
    while half >= HGRN_DIAG:
        blk = 2 * half
        m = jnp.concatenate([_rows_from(b, k * blk + half - 1, blk) for k in range(c // blk)], axis=0)
        decay = jnp.exp(-jnp.abs(b - m))
        second = (rowi & (blk - 1)) >= half
        x = (jnp.where(second, qs, kk) * decay).astype(BF16)
        mask = ((ti ^ si) < blk) & ((ti & half) != 0) & ((si & half) == 0)
        levels.append((x, mask))
        half //= 2
    r = jnp.concatenate([_rows_from(b, k * HGRN_DIAG - 1, HGRN_DIAG) for k in range(c // HGRN_DIAG)], axis=0)
    q_dg = (qs * jnp.exp(b - r)).astype(BF16)
    k_dg = (kk * jnp.exp(jnp.minimum(r - b, EXP_CLAMP))).astype(BF16)
    mask_dg = ((ti ^ si) < HGRN_DIAG) & (si <= ti)
    return dict(q_in=q_in, k_end=k_end, s_decay=s_decay, levels=levels, q_dg=q_dg, k_dg=k_dg, mask_dg=mask_dg)


def _hgrn_intra(p):
    out = []
    for h in range(HGRN_HEADS):
        hs = slice(h * HGRN_DK, (h + 1) * HGRN_DK)
        a = jnp.where(p["mask_dg"],
                      lax.dot_general(p["q_dg"][:, hs], p["k_dg"][:, hs], _NT, preferred_element_type=F32), 0.0)
        for x, mask in p["levels"]:
            xh = x[:, hs]
            a = jnp.where(mask, lax.dot_general(xh, xh, _NT, preferred_element_type=F32), a)
        out.append(a.astype(BF16))
    return out


def _hgrn_apply(p, a_heads, iv, gate_b, gng, st_ref, n, yb_ref, row0):
    c = iv.shape[0]
    for h in range(HGRN_HEADS):
        hs = slice(h * HGRN_DK, (h + 1) * HGRN_DK)
        vs = slice(h * HGRN_DV, (h + 1) * HGRN_DV)
        st = st_ref[n, h]
        o = (lax.dot_general(p["q_in"][:, hs], st.astype(BF16), _NT, preferred_element_type=F32)
             + jnp.dot(a_heads[h], iv[:, vs], preferred_element_type=F32))
        st_ref[n, h] = (st * p["s_decay"][:, hs]
                        + lax.dot_general(iv[:, vs], p["k_end"][:, hs], _TN, preferred_element_type=F32))
        on = o * _rms(o) * gng[:, vs]
        yb_ref[row0:row0 + c, vs] = (on * gate_b[:, vs]).astype(BF16)


def _layer_body(cfg, *refs):
    refs = list(refs)
    x_ref, mod_ref = refs[:2]
    pos = 2
    s0_ref = None
    if cfg.has_s0:
        s0_ref = refs[pos]
        pos += 1
    (normg_ref, win_ref, lnvg_ref, lnvb_ref, ws_ref, bs_ref, lbraw_ref, gng_ref,
     wpa_ref, wpb_ref, wo_ref, fing_ref) = refs[pos:pos + 12]
    pos += 12
    y_ref, sout_ref = refs[pos:pos + 2]
    pos += 2
    v_ref = None
    if cfg.emit_v:
        v_ref = refs[pos]
        pos += 1
    st_ref, yb_ref = refs[pos:pos + 2]

    nb, tt = cfg.nb, cfg.tt
    rows = nb * tt
    d = x_ref.shape[-1]
    t_idx = pl.program_id(1)
    n_t = pl.num_programs(1)

    hs = []
    for n in range(nb):
        x = x_ref[n]
        shift = mod_ref[n, 0:1, :]
        scale = mod_ref[n, 1:2, :]
        hs.append(((x * _rms(x)) * normg_ref[...] * (1.0 + scale) + shift).astype(BF16))
    h = hs[0] if nb == 1 else jnp.concatenate(hs, axis=0)

    def proj(slab):
        return jnp.dot(h, _as_bf16(win_ref[:, slab * d:(slab + 1) * d]), preferred_element_type=F32)

    z = proj(SLAB_F)
    pq = proj(SLAB_Q)
    iv = proj(SLAB_I).astype(BF16)
    pv = proj(SLAB_V)
    pu = proj(SLAB_U)
    pza = proj(SLAB_ZA)

    e = jnp.exp(-jnp.abs(z))
    log_sig = jnp.minimum(z, 0.0) - jnp.log1p(e)
    sig_neg = jnp.where(z >= 0.0, e, 1.0) / (1.0 + e)
    if cfg.layer == 0:
        lf = log_sig
        kk = sig_neg
    else:
        lb = _forget_lower_bound(lbraw_ref, cfg.layer, cfg.depth)
        ya_ = jnp.log1p(-lb) + log_sig
        la = jnp.log(lb)
        lf = jnp.maximum(la, ya_) + jnp.log1p(jnp.exp(-jnp.abs(la - ya_)))
        kk = (1.0 - lb) * sig_neg
    qs = _silu(pq)

    v = _gelu_tanh(pv)
    mu = jnp.mean(v, axis=-1, keepdims=True)
    vc = v - mu
    var = jnp.mean(vc * vc, axis=-1, keepdims=True)
    vn = vc * lax.rsqrt(var + EPS) * lnvg_ref[...] + lnvb_ref[...]
    if cfg.emit_v:
        for n in range(nb):
            v_ref[n] = vn[n * tt:(n + 1) * tt]
    vb = vn.astype(BF16)
    gc = cfg.gc
    gdim = d // GMLP_GROUPS
    tril = (lax.broadcasted_iota(jnp.int32, (gc, gc), 0) >= lax.broadcasted_iota(jnp.int32, (gc, gc), 1))
    w_tril = [jnp.where(tril, ws_ref[g], 0.0).astype(BF16) for g in range(GMLP_GROUPS)]
    s_chunks = []
    for j in range(rows // gc):
        parts = [jnp.dot(w_tril[g], vb[j * gc:(j + 1) * gc, g * gdim:(g + 1) * gdim],
                         preferred_element_type=F32) + bs_ref[:, g:g + 1]
                 for g in range(GMLP_GROUPS)]
        s_chunks.append(jnp.concatenate(parts, axis=1))
    s = s_chunks[0] if len(s_chunks) == 1 else jnp.concatenate(s_chunks, axis=0)

    pzb = proj(SLAB_ZB)
    pga = proj(SLAB_GA)

    hc = cfg.hc
    cps = tt // hc
    n_chunks = rows // hc
    prep = [_hgrn_prepare(qs[j * hc:(j + 1) * hc], kk[j * hc:(j + 1) * hc], lf[j * hc:(j + 1) * hc])
            for j in range(n_chunks)]
    a_heads = [_hgrn_intra(p) for p in prep]

    ya = (_gelu_tanh(pu) * s * _silu(pza)).astype(BF16)
    m_a = _sigmoid(pga) * jnp.dot(ya, _as_bf16(wpa_ref[...]), preferred_element_type=F32)

    gate_b = _silu(pzb)
    gng = gng_ref[...]
    for j in range(n_chunks):
        n = j // cps
        if j % cps == 0:
            @pl.when(t_idx == 0)
            def _():
                for hd in range(HGRN_HEADS):
                    if cfg.has_s0:
                        st_ref[n, hd] = s0_ref[n, hd].T
                    else:
                        st_ref[n, hd] = jnp.zeros((HGRN_DV, HGRN_DK), F32)
        sl = slice(j * hc, (j + 1) * hc)
        _hgrn_apply(prep[j], a_heads[j], iv[sl], gate_b[sl], gng, st_ref, n, yb_ref, j * hc)
        if j % cps == cps - 1:
            @pl.when(t_idx == n_t - 1)
            def _():
                for hd in range(HGRN_HEADS):
                    sout_ref[n, hd] = st_ref[n, hd].T

    pgb = proj(SLAB_GB)
    m_b = _sigmoid(pgb) * jnp.dot(yb_ref[...], _as_bf16(wpb_ref[...]), preferred_element_type=F32)
    upd = jnp.dot((m_a + m_b).astype(BF16), _as_bf16(wo_ref[...]), preferred_element_type=F32)
    for n in range(nb):
        gate = mod_ref[n, 2:3, :]
        xo = x_ref[n] + gate * upd[n * tt:(n + 1) * tt]
        if cfg.final:
            xo = (xo * _rms(xo)) * fing_ref[...]
        y_ref[n] = xo


def _tiling(batch, seq):
    chunk = min(seq, GMLP_CHUNK)
    assert seq % chunk == 0 and chunk % HGRN_DIAG == 0
    if seq >= TILE_ROWS:
        assert seq % TILE_ROWS == 0
        return 1, TILE_ROWS, chunk
    nb = max(1, TILE_ROWS // seq)
    while batch % nb:
        nb -= 1
    return nb, seq, chunk


def _trunk_layer(layer, depth, x, mod, s0, w, final, emit_v):
    batch, seq, d = x.shape
    nb, tt, chunk = _tiling(batch, seq)
    hc = min(chunk, HGRN_CHUNK)
    cfg = _Cfg(layer=layer, depth=depth, nb=nb, tt=tt, gc=chunk, hc=hc,
               has_s0=s0 is not None, emit_v=emit_v, final=final)
    rows = nb * tt

    def full(a):
        nd = a.ndim
        return pl.BlockSpec(a.shape, lambda b, t, _nd=nd: (0,) * _nd)

    tok_spec = pl.BlockSpec((nb, tt, d), lambda b, t: (b, t, 0))
    st_spec = pl.BlockSpec((nb, HGRN_HEADS, HGRN_DK, HGRN_DV), lambda b, t: (b, 0, 0, 0))
    ws = w["w_s"][:, :chunk, :chunk]
    bs_t = w["b_s"][:, :chunk].T
    params = [w["norm_g"], w["w_in"], w["ln_v_g"], w["ln_v_b"], ws, bs_t, w["lb_raw"], w["gnorm_g"],
              w["w_pa"], w["w_pb"], w["w_o"], w["final_g"]]
    args = [x, mod] + ([s0] if s0 is not None else []) + params
    in_specs = ([tok_spec, pl.BlockSpec((nb, 3, d), lambda b, t: (b, 0, 0))]
                + ([st_spec] if s0 is not None else []) + [full(a) for a in params])
    out_shape = [jax.ShapeDtypeStruct(x.shape, F32),
                 jax.ShapeDtypeStruct((batch, HGRN_HEADS, HGRN_DK, HGRN_DV), F32)]
    out_specs = [tok_spec, st_spec]
    if emit_v:
        out_shape.append(jax.ShapeDtypeStruct(x.shape, F32))
        out_specs.append(tok_spec)
    return pl.pallas_call(
        functools.partial(_layer_body, cfg),
        grid=(batch // nb, seq // tt),
        in_specs=in_specs,
        out_specs=out_specs,
        out_shape=out_shape,
        scratch_shapes=[pltpu.VMEM((nb, HGRN_HEADS, HGRN_DV, HGRN_DK), F32),
                        pltpu.VMEM((rows, d), BF16)],
        compiler_params=pltpu.CompilerParams(
            dimension_semantics=("arbitrary", "arbitrary"),
            vmem_limit_bytes=V7X_VMEM_LIMIT_BYTES),
        name=f"trunk_layer{layer}_{'sample' if emit_v else 'prompt'}",
    )(*args)


def _ada_body(c_ref, w_ref, b_ref, o_ref):
    c = c_ref[...]
    o_ref[0] = jnp.dot(_silu(c), w_ref[0], preferred_element_type=F32,
                       precision=lax.Precision.HIGHEST) + b_ref[0]


def _ada_modulation(c_all, w_ada, b_ada):
    depth, d, d3 = w_ada.shape
    n = c_all.shape[0]
    return pl.pallas_call(
        _ada_body,
        grid=(depth, d3 // d),
        in_specs=[pl.BlockSpec((n, d), lambda l, j: (0, 0)),
                  pl.BlockSpec((1, d, d), lambda l, j: (l, 0, j)),
                  pl.BlockSpec((1, 1, d), lambda l, j: (l, 0, j))],
        out_specs=pl.BlockSpec((1, n, d), lambda l, j: (l, 0, j)),
        out_shape=jax.ShapeDtypeStruct((depth, n, d3), F32),
        compiler_params=pltpu.CompilerParams(dimension_semantics=("arbitrary", "arbitrary")),
        name="ada_modulation",
    )(c_all, w_ada, b_ada.reshape(depth, 1, d3))


def kernel(x_prompt, x_sample, state_hgrn, c_prompt, c_sample, w_ada, b_ada, norm_g, w_in, ln_v_g, ln_v_b,
           w_s, b_s, lb_raw, gnorm_g, w_pa, w_pb, w_o, final_g):
    depth, d = norm_g.shape
    n_prompt = x_prompt.shape[0]
    mod = _ada_modulation(jnp.concatenate([c_prompt, c_sample], axis=0), w_ada, b_ada)
    mod = mod.reshape(depth, -1, 3, d)

    xp, xs = x_prompt, x_sample
    sp_list, ss_list, vs_list = [], [], []
    for l in range(depth):
        w = dict(norm_g=norm_g[l][None], w_in=_pack_rows_bf16(w_in[l]), ln_v_g=ln_v_g[l][None],
                 ln_v_b=ln_v_b[l][None], w_s=w_s[l], b_s=b_s[l], lb_raw=lb_raw, gnorm_g=gnorm_g[l][None],
                 w_pa=_pack_rows_bf16(w_pa[l]), w_pb=_pack_rows_bf16(w_pb[l]), w_o=_pack_rows_bf16(w_o[l]),
                 final_g=final_g[None])
        final = l == depth - 1
        xp, sp = _trunk_layer(l, depth, xp, mod[l, :n_prompt], None, w, final, False)
        xs, ss, vs = _trunk_layer(l, depth, xs, mod[l, n_prompt:], state_hgrn[l], w, final, True)
        sp_list.append(sp)
        ss_list.append(ss)
        vs_list.append(vs)
    return (xp, xs, jnp.stack(sp_list), jnp.stack(ss_list), jnp.stack(vs_list))
```

```python
import functools
import math
from typing import NamedTuple

import jax
import jax.numpy as jnp
from jax import lax
from jax.experimental import pallas as pl
from jax.experimental.pallas import tpu as pltpu

F32 = jnp.float32
BF16 = jnp.bfloat16

EPS = 1e-6
LANES = 128
GMLP_CHUNK = 128
GMLP_GROUPS = 4
HGRN_HEADS = 8
HGRN_DK = 128
HGRN_DV = 128
N_IN_SLABS = 9
SLAB_U, SLAB_V, SLAB_ZA, SLAB_Q, SLAB_F, SLAB_I, SLAB_ZB, SLAB_GA, SLAB_GB = range(N_IN_SLABS)

HGRN_CHUNK = 128
HGRN_BLOCKS = 8
EXP_CLAMP = 80.0
TILE_ROWS = 256
PACK_BLOCK_COLS = 1024
V7X_VMEM_LIMIT_BYTES = 60000 * 1024

_NT = (((1,), (1,)), ((), ()))
_TN = (((0,), (0,)), ((), ()))


class _Cfg(NamedTuple):
    layer: int
    depth: int
    nb: int
    tt: int
    gc: int
    hc: int
    has_s0: bool
    emit_v: bool
    final: bool


def _gelu_tanh(x):
    c = math.sqrt(2.0 / math.pi)
    return 0.5 * x * (1.0 + jnp.tanh(c * (x + 0.044715 * (x * x * x))))


def _sigmoid(x):
    return 1.0 / (1.0 + jnp.exp(-x))


def _silu(x):
    return x * _sigmoid(x)


def _rms(x):
    return lax.rsqrt(jnp.mean(x * x, axis=-1, keepdims=True) + EPS)


def _pack_body(w_ref, o_ref):
    o_ref[...] = pltpu.bitcast(w_ref[...].astype(BF16), jnp.uint32)


def _pack_rows_bf16(w_stack, layer):
    _, k, n = w_stack.shape
    bn = min(n, PACK_BLOCK_COLS)
    assert n % bn == 0
    return pl.pallas_call(
        _pack_body,
        grid=(n // bn,),
        in_specs=[pl.BlockSpec((None, k, bn), lambda j: (layer, 0, j))],
        out_specs=pl.BlockSpec((k // 2, bn), lambda j: (0, j)),
        out_shape=jax.ShapeDtypeStruct((k // 2, n), jnp.uint32),
        compiler_params=pltpu.CompilerParams(dimension_semantics=("arbitrary",)),
        name="pack_bf16_rows",
    )(w_stack)


def _as_bf16(packed):
    return pltpu.bitcast(packed, BF16)


def _row_bcast(a, idx, n_rep):
    return jnp.broadcast_to(a[idx:idx + 1, :], (n_rep, a.shape[1]))


def _forget_lower_bound(lbraw_ref, layer, depth):
    raw = lbraw_ref[...]
    mx = jnp.max(raw, axis=0, keepdims=True)
    ex = jnp.exp(raw - mx)
    den = jnp.sum(ex, axis=0, keepdims=True)
    num = jnp.zeros_like(den)
    for j in range(1, layer + 1):
        num = num + ex[j:j + 1, :]
    return num / den


def _hgrn_prepare(qs, kk, lf, loc_ref):
    c, d = qs.shape
    nblk = HGRN_BLOCKS
    dg = c // nblk
    rowi = lax.broadcasted_iota(jnp.int32, (c, d), 0)
    inblk = rowi & (dg - 1)
    loc = lf
    sh = 1
    while sh < dg:
        loc = loc + jnp.where(inblk >= sh, pltpu.roll(loc, sh, 0), 0.0)
        sh *= 2
    n_lane_tiles = d // LANES
    for t in range(n_lane_tiles):
        loc_ref[t] = loc[:, t * LANES:(t + 1) * LANES]
    tot = jnp.concatenate([loc_ref[t, pl.ds(dg - 1, nblk, stride=dg), :] for t in range(n_lane_tiles)],
                          axis=1)
    kidx = lax.broadcasted_iota(jnp.int32, (nblk, d), 0)
    r_end = tot
    sh = 1
    while sh < nblk:
        r_end = r_end + jnp.where(kidx >= sh, pltpu.roll(r_end, sh, 0), 0.0)
        sh *= 2
    r_start = r_end - tot
    b_last = r_end[nblk - 1:nblk, :]

    tot_rows = jnp.concatenate([_row_bcast(tot, k, dg) for k in range(nblk)], axis=0)
    q_hat = qs * jnp.exp(loc)
    k_hat = kk * jnp.exp(tot_rows - loc)
    q_dg = q_hat.astype(BF16)
    k_dg = (kk * jnp.exp(jnp.minimum(-loc, EXP_CLAMP))).astype(BF16)

    def scaled(fac, use_q):
        parts = [(q_hat if use_q[k] else k_hat)[k * dg:(k + 1) * dg] * _row_bcast(fac, k, dg)
                 for k in range(nblk)]
        return jnp.concatenate(parts, axis=0).astype(BF16)

    q_in = scaled(jnp.exp(r_start), [True] * nblk)
    k_end = scaled(jnp.exp(b_last - r_end), [False] * nblk)
    s_decay = jnp.exp(b_last)

    ti = lax.broadcasted_iota(jnp.int32, (c, c), 0)
    si = lax.broadcasted_iota(jnp.int32, (c, c), 1)
    levels = []
    hb = nblk // 2
    while hb >= 1:
        m = None
        for g in reversed(range(nblk // (2 * hb))):
            row = _row_bcast(r_start, g * 2 * hb + hb, nblk)
            m = row if m is None else jnp.where(kidx < (g + 1) * 2 * hb, row, m)
        fac = jnp.exp(jnp.where((kidx & hb) != 0, r_start - m, m - r_end))
        x = scaled(fac, [(k & hb) != 0 for k in range(nblk)])
        half = hb * dg
        mask = ((ti ^ si) < 2 * half) & ((ti & half) != 0) & ((si & half) == 0)
        levels.append((x, mask))
        hb //= 2
    mask_dg = ((ti ^ si) < dg) & (si <= ti)
    return dict(q_in=q_in, k_end=k_end, s_decay=s_decay, levels=levels, q_dg=q_dg, k_dg=k_dg, mask_dg=mask_dg)


def _hgrn_intra(p):
    out = []
    for h in range(HGRN_HEADS):
        hs = slice(h * HGRN_DK, (h + 1) * HGRN_DK)
        a = jnp.where(p["mask_dg"],
                      lax.dot_general(p["q_dg"][:, hs], p["k_dg"][:, hs], _NT, preferred_element_type=F32), 0.0)
        for x, mask in p["levels"]:
            xh = x[:, hs]
            a = jnp.where(mask, lax.dot_general(xh, xh, _NT, preferred_element_type=F32), a)
        out.append(a.astype(BF16))
    return out


def _hgrn_apply(p, a_heads, iv, gate_b, gng, st_ref, n, yb_ref, row0):
    c = iv.shape[0]
    for h in range(HGRN_HEADS):
        hs = slice(h * HGRN_DK, (h + 1) * HGRN_DK)
        vs = slice(h * HGRN_DV, (h + 1) * HGRN_DV)
        st = st_ref[n, h]
        o = (lax.dot_general(p["q_in"][:, hs], st.astype(BF16), _NT, preferred_element_type=F32)
             + jnp.dot(a_heads[h], iv[:, vs], preferred_element_type=F32))
        st_ref[n, h] = (st * p["s_decay"][:, hs]
                        + lax.dot_general(iv[:, vs], p["k_end"][:, hs], _TN, preferred_element_type=F32))
        on = o * _rms(o) * gng[:, vs]
        yb_ref[row0:row0 + c, vs] = (on * gate_b[:, vs]).astype(BF16)


def _layer_body(cfg, *refs):
    refs = list(refs)
    x_ref, mod_ref = refs[:2]
    pos = 2
    s0_ref = None
    if cfg.has_s0:
        s0_ref = refs[pos]
        pos += 1
    (normg_ref, win_ref, lnvg_ref, lnvb_ref, ws_ref, bs_ref, lbraw_ref, gng_ref,
     wpa_ref, wpb_ref, wo_ref, fing_ref) = refs[pos:pos + 12]
    pos += 12
    y_ref, sout_ref = refs[pos:pos + 2]
    pos += 2
    v_ref = None
    if cfg.emit_v:
        v_ref = refs[pos]
        pos += 1
    st_ref, yb_ref, loc_ref = refs[pos:pos + 3]

    nb, tt = cfg.nb, cfg.tt
    rows = nb * tt
    d = x_ref.shape[-1]
    t_idx = pl.program_id(1)
    n_t = pl.num_programs(1)

    hs = []
    for n in range(nb):
        x = x_ref[n]
        shift = mod_ref[n, 0:1, :]
        scale = mod_ref[n, 1:2, :]
        hs.append(((x * _rms(x)) * normg_ref[...] * (1.0 + scale) + shift).astype(BF16))
    h = hs[0] if nb == 1 else jnp.concatenate(hs, axis=0)

    def proj(slab):
        return jnp.dot(h, _as_bf16(win_ref[:, slab * d:(slab + 1) * d]), preferred_element_type=F32)

    z = proj(SLAB_F)
    pq = proj(SLAB_Q)
    iv = proj(SLAB_I).astype(BF16)
    pv = proj(SLAB_V)
    pu = proj(SLAB_U)
    pza = proj(SLAB_ZA)

    e = jnp.exp(-jnp.abs(z))
    inv = 1.0 / (1.0 + e)
    log_sig = jnp.minimum(z, 0.0) - jnp.log(1.0 + e)
    sig_neg = jnp.where(z >= 0.0, e, 1.0) * inv
    if cfg.layer == 0:
        lf = log_sig
        kk = sig_neg
    else:
        lb = _forget_lower_bound(lbraw_ref, cfg.layer, cfg.depth)
        f = lb + (1.0 - lb) * (jnp.where(z >= 0.0, 1.0, e) * inv)
        lf = jnp.where(f > 0.0, jnp.log(f), log_sig)
        kk = (1.0 - lb) * sig_neg
    qs = _silu(pq)

    v = _gelu_tanh(pv)
    mu = jnp.mean(v, axis=-1, keepdims=True)
    vc = v - mu
    var = jnp.mean(vc * vc, axis=-1, keepdims=True)
    vn = vc * lax.rsqrt(var + EPS) * lnvg_ref[...] + lnvb_ref[...]
    if cfg.emit_v:
        for n in range(nb):
            v_ref[n] = vn[n * tt:(n + 1) * tt]
    vb = vn.astype(BF16)
    gc = cfg.gc
    gdim = d // GMLP_GROUPS
    tril = (lax.broadcasted_iota(jnp.int32, (gc, gc), 0) >= lax.broadcasted_iota(jnp.int32, (gc, gc), 1))
    w_tril = [jnp.where(tril, ws_ref[g], 0.0).astype(BF16) for g in range(GMLP_GROUPS)]
    s_chunks = []
    for j in range(rows // gc):
        parts = [jnp.dot(w_tril[g], vb[j * gc:(j + 1) * gc, g * gdim:(g + 1) * gdim],
                         preferred_element_type=F32) + bs_ref[:, g:g + 1]
                 for g in range(GMLP_GROUPS)]
        s_chunks.append(jnp.concatenate(parts, axis=1))
    s = s_chunks[0] if len(s_chunks) == 1 else jnp.concatenate(s_chunks, axis=0)

    pzb = proj(SLAB_ZB)
    pga = proj(SLAB_GA)

    hc = cfg.hc
    cps = tt // hc
    n_chunks = rows // hc
    prep = [_hgrn_prepare(qs[j * hc:(j + 1) * hc], kk[j * hc:(j + 1) * hc], lf[j * hc:(j + 1) * hc],
                          loc_ref.at[j])
            for j in range(n_chunks)]
    a_heads = [_hgrn_intra(p) for p in prep]

    ya = (_gelu_tanh(pu) * s * _silu(pza)).astype(BF16)
    m_a = _sigmoid(pga) * jnp.dot(ya, _as_bf16(wpa_ref[...]), preferred_element_type=F32)

    gate_b = _silu(pzb)
    gng = gng_ref[...]
    for j in range(n_chunks):
        n = j // cps
        if j % cps == 0:
            @pl.when(t_idx == 0)
            def _():
                for hd in range(HGRN_HEADS):
                    if cfg.has_s0:
                        st_ref[n, hd] = s0_ref[n, hd].T
                    else:
                        st_ref[n, hd] = jnp.zeros((HGRN_DV, HGRN_DK), F32)
        sl = slice(j * hc, (j + 1) * hc)
        _hgrn_apply(prep[j], a_heads[j], iv[sl], gate_b[sl], gng, st_ref, n, yb_ref, j * hc)
        if j % cps == cps - 1:
            @pl.when(t_idx == n_t - 1)
            def _():
                for hd in range(HGRN_HEADS):
                    sout_ref[n, hd] = st_ref[n, hd].T

    pgb = proj(SLAB_GB)
    m_b = _sigmoid(pgb) * jnp.dot(yb_ref[...], _as_bf16(wpb_ref[...]), preferred_element_type=F32)
    upd = jnp.dot((m_a + m_b).astype(BF16), _as_bf16(wo_ref[...]), preferred_element_type=F32)
    for n in range(nb):
        gate = mod_ref[n, 2:3, :]
        xo = x_ref[n] + gate * upd[n * tt:(n + 1) * tt]
        if cfg.final:
            xo = (xo * _rms(xo)) * fing_ref[...]
        y_ref[n] = xo


def _tiling(batch, seq):
    chunk = min(seq, GMLP_CHUNK)
    assert seq % chunk == 0 and chunk % (8 * HGRN_BLOCKS) == 0
    if seq >= TILE_ROWS:
        assert seq % TILE_ROWS == 0
        return 1, TILE_ROWS, chunk
    nb = max(1, TILE_ROWS // seq)
    while batch % nb:
        nb -= 1
    return nb, seq, chunk


def _trunk_layer(layer, depth, x, mod, s0, w, final, emit_v):
    batch, seq, d = x.shape
    nb, tt, chunk = _tiling(batch, seq)
    hc = min(chunk, HGRN_CHUNK)
    cfg = _Cfg(layer=layer, depth=depth, nb=nb, tt=tt, gc=chunk, hc=hc,
               has_s0=s0 is not None, emit_v=emit_v, final=final)
    rows = nb * tt

    def full(a):
        nd = a.ndim
        return pl.BlockSpec(a.shape, lambda b, t, _nd=nd: (0,) * _nd)

    tok_spec = pl.BlockSpec((nb, tt, d), lambda b, t: (b, t, 0))
    st_spec = pl.BlockSpec((nb, HGRN_HEADS, HGRN_DK, HGRN_DV), lambda b, t: (b, 0, 0, 0))
    ws = w["w_s"][:, :chunk, :chunk]
    bs_t = w["b_s"][:, :chunk].T
    params = [w["norm_g"], w["w_in"], w["ln_v_g"], w["ln_v_b"], ws, bs_t, w["lb_raw"], w["gnorm_g"],
              w["w_pa"], w["w_pb"], w["w_o"], w["final_g"]]
    args = [x, mod] + ([s0] if s0 is not None else []) + params
    in_specs = ([tok_spec, pl.BlockSpec((nb, 3, d), lambda b, t: (b, 0, 0))]
                + ([st_spec] if s0 is not None else []) + [full(a) for a in params])
    out_shape = [jax.ShapeDtypeStruct(x.shape, F32),
                 jax.ShapeDtypeStruct((batch, HGRN_HEADS, HGRN_DK, HGRN_DV), F32)]
    out_specs = [tok_spec, st_spec]
    if emit_v:
        out_shape.append(jax.ShapeDtypeStruct(x.shape, F32))
        out_specs.append(tok_spec)
    return pl.pallas_call(
        functools.partial(_layer_body, cfg),
        grid=(batch // nb, seq // tt),
        in_specs=in_specs,
        out_specs=out_specs,
        out_shape=out_shape,
        scratch_shapes=[pltpu.VMEM((nb, HGRN_HEADS, HGRN_DV, HGRN_DK), F32),
                        pltpu.VMEM((rows, d), BF16),
                        pltpu.VMEM((rows // hc, d // LANES, hc, LANES), F32)],
        compiler_params=pltpu.CompilerParams(
            dimension_semantics=("arbitrary", "arbitrary"),
            vmem_limit_bytes=V7X_VMEM_LIMIT_BYTES),
        name=f"trunk_layer{layer}_{'sample' if emit_v else 'prompt'}",
    )(*args)


def _ada_body(c_ref, w_ref, b_ref, o_ref):
    c = c_ref[...]
    o_ref[0] = jnp.dot(_silu(c), w_ref[0], preferred_element_type=F32,
                       precision=lax.Precision.HIGHEST) + b_ref[0]


def _ada_modulation(c_all, w_ada, b_ada):
    depth, d, d3 = w_ada.shape
    n = c_all.shape[0]
    return pl.pallas_call(
        _ada_body,
        grid=(depth, d3 // d),
        in_specs=[pl.BlockSpec((n, d), lambda l, j: (0, 0)),
                  pl.BlockSpec((1, d, d), lambda l, j: (l, 0, j)),
                  pl.BlockSpec((1, 1, d), lambda l, j: (l, 0, j))],
        out_specs=pl.BlockSpec((1, n, d), lambda l, j: (l, 0, j)),
        out_shape=jax.ShapeDtypeStruct((depth, n, d3), F32),
        compiler_params=pltpu.CompilerParams(dimension_semantics=("arbitrary", "arbitrary")),
        name="ada_modulation",
    )(c_all, w_ada, b_ada.reshape(depth, 1, d3))


def kernel(x_prompt, x_sample, state_hgrn, c_prompt, c_sample, w_ada, b_ada, norm_g, w_in, ln_v_g, ln_v_b,
           w_s, b_s, lb_raw, gnorm_g, w_pa, w_pb, w_o, final_g):
    depth, d = norm_g.shape
    n_prompt = x_prompt.shape[0]
    mod = _ada_modulation(jnp.concatenate([c_prompt, c_sample], axis=0), w_ada, b_ada)
    mod = mod.reshape(depth, -1, 3, d)

    xp, xs = x_prompt, x_sample
    sp_list, ss_list, vs_list = [], [], []
    for l in range(depth):
        w = dict(norm_g=norm_g[l][None], w_in=_pack_rows_bf16(w_in, l), ln_v_g=ln_v_g[l][None],
                 ln_v_b=ln_v_b[l][None], w_s=w_s[l], b_s=b_s[l], lb_raw=lb_raw, gnorm_g=gnorm_g[l][None],
                 w_pa=_pack_rows_bf16(w_pa, l), w_pb=_pack_rows_bf16(w_pb, l), w_o=_pack_rows_bf16(w_o, l),
                 final_g=final_g[None])
        final = l == depth - 1
        xp, sp = _trunk_layer(l, depth, xp, mod[l, :n_prompt], None, w, final, False)
        xs, ss, vs = _trunk_layer(l, depth, xs, mod[l, n_prompt:], state_hgrn[l], w, final, True)
        sp_list.append(sp)
        ss_list.append(ss)
        vs_list.append(vs)
    return (xp, xs, jnp.stack(sp_list), jnp.stack(ss_list), jnp.stack(vs_list))
```

```python
import functools
import math
from typing import NamedTuple

import jax
import jax.numpy as jnp
from jax import lax
from jax.experimental import pallas as pl
from jax.experimental.pallas import tpu as pltpu

F32 = jnp.float32
BF16 = jnp.bfloat16

EPS = 1e-6
LANES = 128
GMLP_CHUNK = 128
GMLP_GROUPS = 4
HGRN_HEADS = 8
HGRN_DK = 128
HGRN_DV = 128
N_IN_SLABS = 9
SLAB_U, SLAB_V, SLAB_ZA, SLAB_Q, SLAB_F, SLAB_I, SLAB_ZB, SLAB_GA, SLAB_GB = range(N_IN_SLABS)

HGRN_CHUNK = 128
HGRN_BLOCKS = 8
EXP_CLAMP = 80.0
TILE_ROWS = 256
PACK_BLOCK_COLS = 1024
V7X_VMEM_LIMIT_BYTES = 60000 * 1024

_NT = (((1,), (1,)), ((), ()))
_TN = (((0,), (0,)), ((), ()))


class _Cfg(NamedTuple):
    layer: int
    depth: int
    nb: int
    tt: int
    gc: int
    hc: int
    has_s0: bool
    emit_v: bool
    final: bool


def _gelu_tanh(x):
    c = math.sqrt(2.0 / math.pi)
    return 0.5 * x * (1.0 + jnp.tanh(c * (x + 0.044715 * (x * x * x))))


def _sigmoid(x):
    return 1.0 / (1.0 + jnp.exp(-x))


def _silu(x):
    return x * _sigmoid(x)


def _rms(x):
    return lax.rsqrt(jnp.mean(x * x, axis=-1, keepdims=True) + EPS)


def _pack_body(w_ref, o_ref):
    o_ref[...] = pltpu.bitcast(w_ref[...].astype(BF16), jnp.uint32)


def _pack_rows_bf16(w_stack, layer):
    _, k, n = w_stack.shape
    bn = min(n, PACK_BLOCK_COLS)
    assert n % bn == 0
    return pl.pallas_call(
        _pack_body,
        grid=(n // bn,),
        in_specs=[pl.BlockSpec((None, k, bn), lambda j: (layer, 0, j))],
        out_specs=pl.BlockSpec((k // 2, bn), lambda j: (0, j)),
        out_shape=jax.ShapeDtypeStruct((k // 2, n), jnp.uint32),
        compiler_params=pltpu.CompilerParams(dimension_semantics=("arbitrary",)),
        name="pack_bf16_rows",
    )(w_stack)


def _as_bf16(packed):
    return pltpu.bitcast(packed, BF16)


def _row_bcast(a, idx, n_rep):
    return jnp.broadcast_to(a[idx:idx + 1, :], (n_rep, a.shape[1]))


def _forget_lower_bound(lbraw_ref, layer, depth):
    raw = lbraw_ref[...]
    mx = jnp.max(raw, axis=0, keepdims=True)
    ex = jnp.exp(raw - mx)
    den = jnp.sum(ex, axis=0, keepdims=True)
    num = jnp.zeros_like(den)
    for j in range(1, layer + 1):
        num = num + ex[j:j + 1, :]
    return num / den


def _hgrn_prepare(qs, kk, lf, loc_ref):
    c, d = qs.shape
    nblk = HGRN_BLOCKS
    dg = c // nblk
    rowi = lax.broadcasted_iota(jnp.int32, (c, d), 0)
    inblk = rowi & (dg - 1)
    loc = lf
    sh = 1
    while sh < dg:
        loc = loc + jnp.where(inblk >= sh, pltpu.roll(loc, sh, 0), 0.0)
        sh *= 2
    n_lane_tiles = d // LANES
    for t in range(n_lane_tiles):
        loc_ref[t] = loc[:, t * LANES:(t + 1) * LANES]
    tot = jnp.concatenate([loc_ref[t, pl.ds(dg - 1, nblk, stride=dg), :] for t in range(n_lane_tiles)],
                          axis=1)
    kidx = lax.broadcasted_iota(jnp.int32, (nblk, d), 0)
    r_end = tot
    sh = 1
    while sh < nblk:
        r_end = r_end + jnp.where(kidx >= sh, pltpu.roll(r_end, sh, 0), 0.0)
        sh *= 2
    r_start = r_end - tot
    b_last = r_end[nblk - 1:nblk, :]

    tot_rows = jnp.concatenate([_row_bcast(tot, k, dg) for k in range(nblk)], axis=0)
    q_hat = qs * jnp.exp(loc)
    k_hat = kk * jnp.exp(tot_rows - loc)
    q_dg = q_hat.astype(BF16)
    k_dg = (kk * jnp.exp(jnp.minimum(-loc, EXP_CLAMP))).astype(BF16)

    def scaled(fac, use_q):
        parts = [(q_hat if use_q[k] else k_hat)[k * dg:(k + 1) * dg] * _row_bcast(fac, k, dg)
                 for k in range(nblk)]
        return jnp.concatenate(parts, axis=0).astype(BF16)

    q_in = scaled(jnp.exp(r_start), [True] * nblk)
    k_end = scaled(jnp.exp(b_last - r_end), [False] * nblk)
    s_decay = jnp.exp(b_last)

    ti = lax.broadcasted_iota(jnp.int32, (c, c), 0)
    si = lax.broadcasted_iota(jnp.int32, (c, c), 1)
    levels = []
    hb = nblk // 2
    while hb >= 1:
        m = None
        for g in reversed(range(nblk // (2 * hb))):
            row = _row_bcast(r_start, g * 2 * hb + hb, nblk)
            m = row if m is None else jnp.where(kidx < (g + 1) * 2 * hb, row, m)
        fac = jnp.exp(jnp.where((kidx & hb) != 0, r_start - m, m - r_end))
        x = scaled(fac, [(k & hb) != 0 for k in range(nblk)])
        half = hb * dg
        mask = ((ti ^ si) < 2 * half) & ((ti & half) != 0) & ((si & half) == 0)
        levels.append((x, mask))
        hb //= 2
    worst = jnp.max(jnp.max(-tot, axis=1, keepdims=True), axis=0, keepdims=True)
    mask_dg = ((ti ^ si) < dg) & (si <= ti) & (worst <= EXP_CLAMP)
    return dict(q_in=q_in, k_end=k_end, s_decay=s_decay, levels=levels, q_dg=q_dg, k_dg=k_dg, mask_dg=mask_dg,
                flagged=worst[0, 0] > EXP_CLAMP, loc=loc, qs=qs, kk=kk)


def _hgrn_intra(p):
    out = []
    for h in range(HGRN_HEADS):
        hs = slice(h * HGRN_DK, (h + 1) * HGRN_DK)
        a = jnp.where(p["mask_dg"],
                      lax.dot_general(p["q_dg"][:, hs], p["k_dg"][:, hs], _NT, preferred_element_type=F32), 0.0)
        for x, mask in p["levels"]:
            xh = x[:, hs]
            a = jnp.where(mask, lax.dot_general(xh, xh, _NT, preferred_element_type=F32), a)
        out.append(a.astype(BF16))
    return out


def _hgrn_apply(p, a_heads, iv, st_ref, n, o_ref):
    for h in range(HGRN_HEADS):
        hs = slice(h * HGRN_DK, (h + 1) * HGRN_DK)
        vs = slice(h * HGRN_DV, (h + 1) * HGRN_DV)
        st = st_ref[n, h]
        o_ref[:, vs] = (lax.dot_general(p["q_in"][:, hs], st.astype(BF16), _NT, preferred_element_type=F32)
                        + jnp.dot(a_heads[h], iv[:, vs], preferred_element_type=F32))
        st_ref[n, h] = (st * p["s_decay"][:, hs]
                        + lax.dot_general(iv[:, vs], p["k_end"][:, hs], _TN, preferred_element_type=F32))


def _hgrn_exact_diag(p, iv):
    qs, kk, loc = p["qs"], p["kk"], p["loc"]
    c, d = qs.shape
    dg = c // HGRN_BLOCKS
    ivf = iv.astype(F32)
    inblk = lax.broadcasted_iota(jnp.int32, (c, d), 0) & (dg - 1)

    def body(dd, acc):
        valid = inblk >= dd
        expo = jnp.where(valid, loc - pltpu.roll(loc, dd, 0), 0.0)
        w = jnp.where(valid, qs * pltpu.roll(kk, dd, 0) * jnp.exp(expo), 0.0)
        ir = pltpu.roll(ivf, dd, 0)
        parts = []
        for h in range(HGRN_HEADS):
            hs = slice(h * HGRN_DK, (h + 1) * HGRN_DK)
            vs = slice(h * HGRN_DV, (h + 1) * HGRN_DV)
            parts.append(jnp.sum(w[:, hs], axis=-1, keepdims=True) * ir[:, vs])
        return acc + jnp.concatenate(parts, axis=1)

    return lax.fori_loop(0, dg, body, jnp.zeros((c, d), F32))


def _layer_body(cfg, *refs):
    refs = list(refs)
    x_ref, mod_ref = refs[:2]
    pos = 2
    s0_ref = None
    if cfg.has_s0:
        s0_ref = refs[pos]
        pos += 1
    (normg_ref, win_ref, lnvg_ref, lnvb_ref, ws_ref, bs_ref, lbraw_ref, gng_ref,
     wpa_ref, wpb_ref, wo_ref, fing_ref) = refs[pos:pos + 12]
    pos += 12
    y_ref, sout_ref = refs[pos:pos + 2]
    pos += 2
    v_ref = None
    if cfg.emit_v:
        v_ref = refs[pos]
        pos += 1
    st_ref, yb_ref, loc_ref, o_ref = refs[pos:pos + 4]

    nb, tt = cfg.nb, cfg.tt
    rows = nb * tt
    d = x_ref.shape[-1]
    t_idx = pl.program_id(1)
    n_t = pl.num_programs(1)

    @pl.when(t_idx == 0)
    def _():
        for n in range(nb):
            for hd in range(HGRN_HEADS):
                if cfg.has_s0:
                    st_ref[n, hd] = s0_ref[n, hd].T
                else:
                    st_ref[n, hd] = jnp.zeros((HGRN_DV, HGRN_DK), F32)

    hs = []
    for n in range(nb):
        x = x_ref[n]
        shift = mod_ref[n, 0:1, :]
        scale = mod_ref[n, 1:2, :]
        hs.append(((x * _rms(x)) * normg_ref[...] * (1.0 + scale) + shift).astype(BF16))
    h = hs[0] if nb == 1 else jnp.concatenate(hs, axis=0)

    def proj(slab):
        return jnp.dot(h, _as_bf16(win_ref[:, slab * d:(slab + 1) * d]), preferred_element_type=F32)

    z = proj(SLAB_F)
    pq = proj(SLAB_Q)
    iv = proj(SLAB_I).astype(BF16)
    pv = proj(SLAB_V)
    pu = proj(SLAB_U)
    pza = proj(SLAB_ZA)

    e = jnp.exp(-jnp.abs(z))
    inv = 1.0 / (1.0 + e)
    log_sig = jnp.minimum(z, 0.0) - jnp.log(1.0 + e)
    sig_neg = jnp.where(z >= 0.0, e, 1.0) * inv
    if cfg.layer == 0:
        lf = log_sig
        kk = sig_neg
    else:
        lb = _forget_lower_bound(lbraw_ref, cfg.layer, cfg.depth)
        f = lb + (1.0 - lb) * (jnp.where(z >= 0.0, 1.0, e) * inv)
        lf = jnp.where(f > 0.0, jnp.log(f), log_sig)
        kk = (1.0 - lb) * sig_neg
    qs = _silu(pq)

    v = _gelu_tanh(pv)
    mu = jnp.mean(v, axis=-1, keepdims=True)
    vc = v - mu
    var = jnp.mean(vc * vc, axis=-1, keepdims=True)
    vn = vc * lax.rsqrt(var + EPS) * lnvg_ref[...] + lnvb_ref[...]
    if cfg.emit_v:
        for n in range(nb):
            v_ref[n] = vn[n * tt:(n + 1) * tt]
    vb = vn.astype(BF16)
    gc = cfg.gc
    gdim = d // GMLP_GROUPS
    tril = (lax.broadcasted_iota(jnp.int32, (gc, gc), 0) >= lax.broadcasted_iota(jnp.int32, (gc, gc), 1))
    w_tril = [jnp.where(tril, ws_ref[g], 0.0).astype(BF16) for g in range(GMLP_GROUPS)]
    s_chunks = []
    for j in range(rows // gc):
        parts = [jnp.dot(w_tril[g], vb[j * gc:(j + 1) * gc, g * gdim:(g + 1) * gdim],
                         preferred_element_type=F32) + bs_ref[:, g:g + 1]
                 for g in range(GMLP_GROUPS)]
        s_chunks.append(jnp.concatenate(parts, axis=1))
    s = s_chunks[0] if len(s_chunks) == 1 else jnp.concatenate(s_chunks, axis=0)

    pzb = proj(SLAB_ZB)
    pga = proj(SLAB_GA)

    hc = cfg.hc
    cps = tt // hc
    n_chunks = rows // hc
    prep = [_hgrn_prepare(qs[j * hc:(j + 1) * hc], kk[j * hc:(j + 1) * hc], lf[j * hc:(j + 1) * hc],
                          loc_ref.at[j])
            for j in range(n_chunks)]
    a_heads = [_hgrn_intra(p) for p in prep]

    ya = (_gelu_tanh(pu) * s * _silu(pza)).astype(BF16)
    m_a = _sigmoid(pga) * jnp.dot(ya, _as_bf16(wpa_ref[...]), preferred_element_type=F32)

    gate_b = _silu(pzb)
    gng = gng_ref[...]
    for j in range(n_chunks):
        _hgrn_apply(prep[j], a_heads[j], iv[j * hc:(j + 1) * hc], st_ref, j // cps, o_ref.at[j])
    sig_gb = _sigmoid(proj(SLAB_GB))

    def finish():
        for j in range(n_chunks):
            for hd in range(HGRN_HEADS):
                vs = slice(hd * HGRN_DV, (hd + 1) * HGRN_DV)
                o = o_ref[j, :, vs]
                yb_ref[j * hc:(j + 1) * hc, vs] = (o * _rms(o) * gng[:, vs]
                                                   * gate_b[j * hc:(j + 1) * hc, vs]).astype(BF16)
        m_b = sig_gb * jnp.dot(yb_ref[...], _as_bf16(wpb_ref[...]), preferred_element_type=F32)
        upd = jnp.dot((m_a + m_b).astype(BF16), _as_bf16(wo_ref[...]), preferred_element_type=F32)
        for n in range(nb):
            gate = mod_ref[n, 2:3, :]
            xo = x_ref[n] + gate * upd[n * tt:(n + 1) * tt]
            if cfg.final:
                xo = (xo * _rms(xo)) * fing_ref[...]
            y_ref[n] = xo

    finish()

    any_flagged = prep[0]["flagged"]
    for p in prep[1:]:
        any_flagged = jnp.logical_or(any_flagged, p["flagged"])

    @pl.when(any_flagged)
    def _():
        for j in range(n_chunks):
            @pl.when(prep[j]["flagged"])
            def _():
                o_ref[j] = o_ref[j] + _hgrn_exact_diag(prep[j], iv[j * hc:(j + 1) * hc])
        finish()

    @pl.when(t_idx == n_t - 1)
    def _():
        for n in range(nb):
            for hd in range(HGRN_HEADS):
                sout_ref[n, hd] = st_ref[n, hd].T


def _tiling(batch, seq):
    chunk = min(seq, GMLP_CHUNK)
    assert seq % chunk == 0 and chunk % (8 * HGRN_BLOCKS) == 0
    if seq >= TILE_ROWS:
        assert seq % TILE_ROWS == 0
        return 1, TILE_ROWS, chunk
    nb = max(1, TILE_ROWS // seq)
    while batch % nb:
        nb -= 1
    return nb, seq, chunk


def _trunk_layer(layer, depth, x, mod, s0, w, final, emit_v):
    batch, seq, d = x.shape
    nb, tt, chunk = _tiling(batch, seq)
    hc = min(chunk, HGRN_CHUNK)
    cfg = _Cfg(layer=layer, depth=depth, nb=nb, tt=tt, gc=chunk, hc=hc,
               has_s0=s0 is not None, emit_v=emit_v, final=final)
    rows = nb * tt

    def full(a):
        nd = a.ndim
        return pl.BlockSpec(a.shape, lambda b, t, _nd=nd: (0,) * _nd)

    tok_spec = pl.BlockSpec((nb, tt, d), lambda b, t: (b, t, 0))
    st_spec = pl.BlockSpec((nb, HGRN_HEADS, HGRN_DK, HGRN_DV), lambda b, t: (b, 0, 0, 0))
    ws = w["w_s"][:, :chunk, :chunk]
    bs_t = w["b_s"][:, :chunk].T
    params = [w["norm_g"], w["w_in"], w["ln_v_g"], w["ln_v_b"], ws, bs_t, w["lb_raw"], w["gnorm_g"],
              w["w_pa"], w["w_pb"], w["w_o"], w["final_g"]]
    args = [x, mod] + ([s0] if s0 is not None else []) + params
    in_specs = ([tok_spec, pl.BlockSpec((nb, 3, d), lambda b, t: (b, 0, 0))]
                + ([st_spec] if s0 is not None else []) + [full(a) for a in params])
    out_shape = [jax.ShapeDtypeStruct(x.shape, F32),
                 jax.ShapeDtypeStruct((batch, HGRN_HEADS, HGRN_DK, HGRN_DV), F32)]
    out_specs = [tok_spec, st_spec]
    if emit_v:
        out_shape.append(jax.ShapeDtypeStruct(x.shape, F32))
        out_specs.append(tok_spec)
    return pl.pallas_call(
        functools.partial(_layer_body, cfg),
        grid=(batch // nb, seq // tt),
        in_specs=in_specs,
        out_specs=out_specs,
        out_shape=out_shape,
        scratch_shapes=[pltpu.VMEM((nb, HGRN_HEADS, HGRN_DV, HGRN_DK), F32),
                        pltpu.VMEM((rows, d), BF16),
                        pltpu.VMEM((rows // hc, d // LANES, hc, LANES), F32),
                        pltpu.VMEM((rows // hc, hc, d), F32)],
        compiler_params=pltpu.CompilerParams(
            dimension_semantics=("arbitrary", "arbitrary"),
            vmem_limit_bytes=V7X_VMEM_LIMIT_BYTES),
        name=f"trunk_layer{layer}_{'sample' if emit_v else 'prompt'}",
    )(*args)


def _ada_body(c_ref, w_ref, b_ref, o_ref):
    c = c_ref[...]
    o_ref[0] = jnp.dot(_silu(c), w_ref[0], preferred_element_type=F32,
                       precision=lax.Precision.HIGHEST) + b_ref[0]


def _ada_modulation(c_all, w_ada, b_ada):
    depth, d, d3 = w_ada.shape
    n = c_all.shape[0]
    return pl.pallas_call(
        _ada_body,
        grid=(depth, d3 // d),
        in_specs=[pl.BlockSpec((n, d), lambda l, j: (0, 0)),
                  pl.BlockSpec((1, d, d), lambda l, j: (l, 0, j)),
                  pl.BlockSpec((1, 1, d), lambda l, j: (l, 0, j))],
        out_specs=pl.BlockSpec((1, n, d), lambda l, j: (l, 0, j)),
        out_shape=jax.ShapeDtypeStruct((depth, n, d3), F32),
        compiler_params=pltpu.CompilerParams(dimension_semantics=("arbitrary", "arbitrary")),
        name="ada_modulation",
    )(c_all, w_ada, b_ada.reshape(depth, 1, d3))


def kernel(x_prompt, x_sample, state_hgrn, c_prompt, c_sample, w_ada, b_ada, norm_g, w_in, ln_v_g, ln_v_b,
           w_s, b_s, lb_raw, gnorm_g, w_pa, w_pb, w_o, final_g):
    depth, d = norm_g.shape
    n_prompt = x_prompt.shape[0]
    mod = _ada_modulation(jnp.concatenate([c_prompt, c_sample], axis=0), w_ada, b_ada)
    mod = mod.reshape(depth, -1, 3, d)

    xp, xs = x_prompt, x_sample
    sp_list, ss_list, vs_list = [], [], []
    for l in range(depth):
        w = dict(norm_g=norm_g[l][None], w_in=_pack_rows_bf16(w_in, l), ln_v_g=ln_v_g[l][None],
                 ln_v_b=ln_v_b[l][None], w_s=w_s[l], b_s=b_s[l], lb_raw=lb_raw, gnorm_g=gnorm_g[l][None],
                 w_pa=_pack_rows_bf16(w_pa, l), w_pb=_pack_rows_bf16(w_pb, l), w_o=_pack_rows_bf16(w_o, l),
                 final_g=final_g[None])
        final = l == depth - 1
        xp, sp = _trunk_layer(l, depth, xp, mod[l, :n_prompt], None, w, final, False)
        xs, ss, vs = _trunk_layer(l, depth, xs, mod[l, n_prompt:], state_hgrn[l], w, final, True)
        sp_list.append(sp)
        ss_list.append(ss)
        vs_list.append(vs)
    return (xp, xs, jnp.stack(sp_list), jnp.stack(ss_list), jnp.stack(vs_list))
```

```python
import functools
import math
from typing import NamedTuple

import jax
import jax.numpy as jnp
from jax import lax
from jax.experimental import pallas as pl
from jax.experimental.pallas import tpu as pltpu

F32 = jnp.float32
BF16 = jnp.bfloat16

EPS = 1e-6
LANES = 128
GMLP_CHUNK = 128
GMLP_GROUPS = 4
HGRN_HEADS = 8
HGRN_DK = 128
HGRN_DV = 128
N_IN_SLABS = 9
SLAB_U, SLAB_V, SLAB_ZA, SLAB_Q, SLAB_F, SLAB_I, SLAB_ZB, SLAB_GA, SLAB_GB = range(N_IN_SLABS)

HGRN_CHUNK = 128
HGRN_BLOCKS = 4
SUBLANES = 8
EXP2_CLAMP = 115.0
LOG2E = math.log2(math.e)
TILE_ROWS = 256
PACK_BLOCK_COLS = 1024
V7X_VMEM_LIMIT_BYTES = 60000 * 1024

_NT = (((1,), (1,)), ((), ()))
_TN = (((0,), (0,)), ((), ()))


class _Cfg(NamedTuple):
    layer: int
    depth: int
    nb: int
    tt: int
    gc: int
    hc: int
    has_s0: bool
    emit_v: bool
    final: bool


def _gelu_tanh(x):
    c = math.sqrt(2.0 / math.pi)
    return 0.5 * x * (1.0 + jnp.tanh(c * (x + 0.044715 * (x * x * x))))


def _exp_neg(x):
    return jnp.exp2(x * (-LOG2E))


def _sigmoid(x):
    return 1.0 / (1.0 + _exp_neg(x))


def _silu(x):
    return x * _sigmoid(x)


def _rms(x):
    return lax.rsqrt(jnp.mean(x * x, axis=-1, keepdims=True) + EPS)


def _pack_body(w_ref, o_ref):
    o_ref[...] = pltpu.bitcast(w_ref[...].astype(BF16), jnp.uint32)


def _pack_rows_bf16(w_stack, layer):
    _, k, n = w_stack.shape
    bn = min(n, PACK_BLOCK_COLS)
    assert n % bn == 0
    return pl.pallas_call(
        _pack_body,
        grid=(n // bn,),
        in_specs=[pl.BlockSpec((None, k, bn), lambda j: (layer, 0, j))],
        out_specs=pl.BlockSpec((k // 2, bn), lambda j: (0, j)),
        out_shape=jax.ShapeDtypeStruct((k // 2, n), jnp.uint32),
        compiler_params=pltpu.CompilerParams(dimension_semantics=("arbitrary",)),
        name="pack_bf16_rows",
    )(w_stack)


def _as_bf16(packed):
    return pltpu.bitcast(packed, BF16)


def _row_bcast(a, idx, n_rep):
    return jnp.broadcast_to(a[idx:idx + 1, :], (n_rep, a.shape[1]))


def _forget_lower_bound(lbraw_ref, layer, depth):
    raw = lbraw_ref[...]
    mx = jnp.max(raw, axis=0, keepdims=True)
    ex = jnp.exp(raw - mx)
    den = jnp.sum(ex, axis=0, keepdims=True)
    num = jnp.zeros_like(den)
    for j in range(1, layer + 1):
        num = num + ex[j:j + 1, :]
    return num / den


def _hgrn_prepare(qs, kk, lf2, loc_ref):
    c, d = qs.shape
    nblk = HGRN_BLOCKS
    dg = c // nblk
    rep = SUBLANES // nblk
    rowi = lax.broadcasted_iota(jnp.int32, (c, d), 0)
    inblk = rowi & (dg - 1)
    loc = lf2
    sh = 1
    while sh < dg:
        loc = loc + jnp.where(inblk >= sh, pltpu.roll(loc, sh, 0), 0.0)
        sh *= 2
    n_lane_tiles = d // LANES
    for t in range(n_lane_tiles):
        loc_ref[t] = loc[:, t * LANES:(t + 1) * LANES]
    step = dg // rep
    part = jnp.concatenate([loc_ref[t, pl.ds(step - 1, SUBLANES, stride=step), :]
                            for t in range(n_lane_tiles)], axis=1)
    kidx = lax.broadcasted_iota(jnp.int32, (SUBLANES, d), 0)
    tot = part
    sh = 1
    while sh < rep:
        tot = jnp.where((kidx & (rep - 1)) >= rep - sh, tot, pltpu.roll(tot, SUBLANES - sh, 0))
        sh *= 2
    r_end = tot
    sh = rep
    while sh < SUBLANES:
        r_end = r_end + jnp.where(kidx >= sh, pltpu.roll(r_end, sh, 0), 0.0)
        sh *= 2
    r_start = r_end - tot
    b_last = r_end[SUBLANES - 1:SUBLANES, :]

    tot_rows = jnp.concatenate([_row_bcast(tot, j * rep, dg) for j in range(nblk)], axis=0)
    q_hat = qs * jnp.exp2(loc)
    k_hat = kk * jnp.exp2(tot_rows - loc)
    q_dg = q_hat.astype(BF16)
    k_dg = (kk * jnp.exp2(jnp.minimum(-loc, EXP2_CLAMP))).astype(BF16)

    def scaled(fac, use_q):
        parts = [(q_hat if use_q[j] else k_hat)[j * dg:(j + 1) * dg] * _row_bcast(fac, j * rep, dg)
                 for j in range(nblk)]
        return jnp.concatenate(parts, axis=0).astype(BF16)

    q_in = scaled(jnp.exp2(r_start), [True] * nblk)
    k_end = scaled(jnp.exp2(b_last - r_end), [False] * nblk)
    s_decay = jnp.exp2(b_last)

    ti = lax.broadcasted_iota(jnp.int32, (c, c), 0)
    si = lax.broadcasted_iota(jnp.int32, (c, c), 1)
    levels = []
    hb = nblk // 2
    while hb >= 1:
        m = None
        for g in reversed(range(nblk // (2 * hb))):
            row = _row_bcast(r_start, (g * 2 * hb + hb) * rep, SUBLANES)
            m = row if m is None else jnp.where(kidx < (g + 1) * 2 * hb * rep, row, m)
        fac = jnp.exp2(jnp.where((kidx & (hb * rep)) != 0, r_start - m, m - r_end))
        x = scaled(fac, [(j & hb) != 0 for j in range(nblk)])
        half = hb * dg
        mask = ((ti ^ si) < 2 * half) & ((ti & half) != 0) & ((si & half) == 0)
        levels.append((x, mask))
        hb //= 2
    worst = jnp.max(jnp.max(-tot, axis=1, keepdims=True), axis=0, keepdims=True)
    mask_dg = ((ti ^ si) < dg) & (si <= ti) & (worst <= EXP2_CLAMP)
    return dict(q_in=q_in, k_end=k_end, s_decay=s_decay, levels=levels, q_dg=q_dg, k_dg=k_dg, mask_dg=mask_dg,
                flagged=worst[0, 0] > EXP2_CLAMP, loc=loc, qs=qs, kk=kk)


def _hgrn_intra(p):
    out = []
    for h in range(HGRN_HEADS):
        hs = slice(h * HGRN_DK, (h + 1) * HGRN_DK)
        a = jnp.where(p["mask_dg"],
                      lax.dot_general(p["q_dg"][:, hs], p["k_dg"][:, hs], _NT, preferred_element_type=F32), 0.0)
        for x, mask in p["levels"]:
            xh = x[:, hs]
            a = jnp.where(mask, lax.dot_general(xh, xh, _NT, preferred_element_type=F32), a)
        out.append(a.astype(BF16))
    return out


def _hgrn_apply(p, a_heads, iv, st_ref, n, o_ref):
    for h in range(HGRN_HEADS):
        hs = slice(h * HGRN_DK, (h + 1) * HGRN_DK)
        vs = slice(h * HGRN_DV, (h + 1) * HGRN_DV)
        st = st_ref[n, h]
        o_ref[:, vs] = (lax.dot_general(p["q_in"][:, hs], st.astype(BF16), _NT, preferred_element_type=F32)
                        + jnp.dot(a_heads[h], iv[:, vs], preferred_element_type=F32))
        st_ref[n, h] = (st * p["s_decay"][:, hs]
                        + lax.dot_general(iv[:, vs], p["k_end"][:, hs], _TN, preferred_element_type=F32))


def _hgrn_exact_diag(p, iv):
    qs, kk, loc = p["qs"], p["kk"], p["loc"]
    c, d = qs.shape
    dg = c // HGRN_BLOCKS
    ivf = iv.astype(F32)
    inblk = lax.broadcasted_iota(jnp.int32, (c, d), 0) & (dg - 1)

    def body(dd, acc):
        valid = inblk >= dd
        expo = jnp.where(valid, loc - pltpu.roll(loc, dd, 0), 0.0)
        w = jnp.where(valid, qs * pltpu.roll(kk, dd, 0) * jnp.exp2(expo), 0.0)
        ir = pltpu.roll(ivf, dd, 0)
        parts = []
        for h in range(HGRN_HEADS):
            hs = slice(h * HGRN_DK, (h + 1) * HGRN_DK)
            vs = slice(h * HGRN_DV, (h + 1) * HGRN_DV)
            parts.append(jnp.sum(w[:, hs], axis=-1, keepdims=True) * ir[:, vs])
        return acc + jnp.concatenate(parts, axis=1)

    return lax.fori_loop(0, dg, body, jnp.zeros((c, d), F32))


def _layer_body(cfg, *refs):
    refs = list(refs)
    x_ref, mod_ref = refs[:2]
    pos = 2
    s0_ref = None
    if cfg.has_s0:
        s0_ref = refs[pos]
        pos += 1
    (normg_ref, win_ref, lnvg_ref, lnvb_ref, ws_ref, bs_ref, lbraw_ref, gng_ref,
     wpa_ref, wpb_ref, wo_ref, fing_ref) = refs[pos:pos + 12]
    pos += 12
    y_ref, sout_ref = refs[pos:pos + 2]
    pos += 2
    v_ref = None
    if cfg.emit_v:
        v_ref = refs[pos]
        pos += 1
    st_ref, yb_ref, loc_ref, o_ref = refs[pos:pos + 4]

    nb, tt = cfg.nb, cfg.tt
    rows = nb * tt
    d = x_ref.shape[-1]
    t_idx = pl.program_id(1)
    n_t = pl.num_programs(1)

    @pl.when(t_idx == 0)
    def _():
        for n in range(nb):
            for hd in range(HGRN_HEADS):
                if cfg.has_s0:
                    st_ref[n, hd] = s0_ref[n, hd].T
                else:
                    st_ref[n, hd] = jnp.zeros((HGRN_DV, HGRN_DK), F32)

    hs = []
    for n in range(nb):
        x = x_ref[n]
        shift = mod_ref[n, 0:1, :]
        scale = mod_ref[n, 1:2, :]
        hs.append(((x * _rms(x)) * (normg_ref[...] * (1.0 + scale)) + shift).astype(BF16))
    h = hs[0] if nb == 1 else jnp.concatenate(hs, axis=0)

    def proj(slab):
        return jnp.dot(h, _as_bf16(win_ref[:, slab * d:(slab + 1) * d]), preferred_element_type=F32)

    z = proj(SLAB_F)
    pq = proj(SLAB_Q)
    iv = proj(SLAB_I).astype(BF16)
    pv = proj(SLAB_V)
    pu = proj(SLAB_U)
    pza = proj(SLAB_ZA)

    e = _exp_neg(jnp.abs(z))
    inv = 1.0 / (1.0 + e)
    log2_sig = jnp.minimum(z, 0.0) * LOG2E - jnp.log2(1.0 + e)
    sig_neg = jnp.where(z >= 0.0, e, 1.0) * inv
    if cfg.layer == 0:
        lf2 = log2_sig
        kk = sig_neg
    else:
        lb = _forget_lower_bound(lbraw_ref, cfg.layer, cfg.depth)
        f = lb + (1.0 - lb) * (jnp.where(z >= 0.0, 1.0, e) * inv)
        lf2 = jnp.where(f > 0.0, jnp.log2(f), log2_sig)
        kk = (1.0 - lb) * sig_neg
    qs = _silu(pq)

    v = _gelu_tanh(pv)
    mu = jnp.mean(v, axis=-1, keepdims=True)
    vc = v - mu
    var = jnp.mean(vc * vc, axis=-1, keepdims=True)
    vn = vc * lax.rsqrt(var + EPS) * lnvg_ref[...] + lnvb_ref[...]
    if cfg.emit_v:
        for n in range(nb):
            v_ref[n] = vn[n * tt:(n + 1) * tt]
    vb = vn.astype(BF16)
    gc = cfg.gc
    gdim = d // GMLP_GROUPS
    tril = (lax.broadcasted_iota(jnp.int32, (gc, gc), 0) >= lax.broadcasted_iota(jnp.int32, (gc, gc), 1))
    w_tril = [jnp.where(tril, ws_ref[g], 0.0).astype(BF16) for g in range(GMLP_GROUPS)]
    s_chunks = []
    for j in range(rows // gc):
        parts = [jnp.dot(w_tril[g], vb[j * gc:(j + 1) * gc, g * gdim:(g + 1) * gdim],
                         preferred_element_type=F32) + bs_ref[:, g:g + 1]
                 for g in range(GMLP_GROUPS)]
        s_chunks.append(jnp.concatenate(parts, axis=1))
    s = s_chunks[0] if len(s_chunks) == 1 else jnp.concatenate(s_chunks, axis=0)

    pzb = proj(SLAB_ZB)
    pga = proj(SLAB_GA)

    hc = cfg.hc
    cps = tt // hc
    n_chunks = rows // hc
    prep = [_hgrn_prepare(qs[j * hc:(j + 1) * hc], kk[j * hc:(j + 1) * hc], lf2[j * hc:(j + 1) * hc],
                          loc_ref.at[j])
            for j in range(n_chunks)]
    a_heads = [_hgrn_intra(p) for p in prep]

    ya = (_gelu_tanh(pu) * s * _silu(pza)).astype(BF16)
    m_a = _sigmoid(pga) * jnp.dot(ya, _as_bf16(wpa_ref[...]), preferred_element_type=F32)

    gate_b = _silu(pzb)
    gng = gng_ref[...]
    for j in range(n_chunks):
        _hgrn_apply(prep[j], a_heads[j], iv[j * hc:(j + 1) * hc], st_ref, j // cps, o_ref.at[j])
    sig_gb = _sigmoid(proj(SLAB_GB))

    def finish():
        for j in range(n_chunks):
            for hd in range(HGRN_HEADS):
                vs = slice(hd * HGRN_DV, (hd + 1) * HGRN_DV)
                o = o_ref[j, :, vs]
                yb_ref[j * hc:(j + 1) * hc, vs] = (o * _rms(o) * gng[:, vs]
                                                   * gate_b[j * hc:(j + 1) * hc, vs]).astype(BF16)
        m_b = sig_gb * jnp.dot(yb_ref[...], _as_bf16(wpb_ref[...]), preferred_element_type=F32)
        upd = jnp.dot((m_a + m_b).astype(BF16), _as_bf16(wo_ref[...]), preferred_element_type=F32)
        for n in range(nb):
            gate = mod_ref[n, 2:3, :]
            xo = x_ref[n] + gate * upd[n * tt:(n + 1) * tt]
            if cfg.final:
                xo = (xo * _rms(xo)) * fing_ref[...]
            y_ref[n] = xo

    finish()

    any_flagged = prep[0]["flagged"]
    for p in prep[1:]:
        any_flagged = jnp.logical_or(any_flagged, p["flagged"])

    @pl.when(any_flagged)
    def _():
        for j in range(n_chunks):
            @pl.when(prep[j]["flagged"])
            def _():
                o_ref[j] = o_ref[j] + _hgrn_exact_diag(prep[j], iv[j * hc:(j + 1) * hc])
        finish()

    @pl.when(t_idx == n_t - 1)
    def _():
        for n in range(nb):
            for hd in range(HGRN_HEADS):
                sout_ref[n, hd] = st_ref[n, hd].T


def _tiling(batch, seq):
    chunk = min(seq, GMLP_CHUNK)
    assert seq % chunk == 0 and chunk % (8 * HGRN_BLOCKS) == 0
    if seq >= TILE_ROWS:
        assert seq % TILE_ROWS == 0
        return 1, TILE_ROWS, chunk
    nb = max(1, TILE_ROWS // seq)
    while batch % nb:
        nb -= 1
    return nb, seq, chunk


def _trunk_layer(layer, depth, x, mod, s0, w, final, emit_v):
    batch, seq, d = x.shape
    nb, tt, chunk = _tiling(batch, seq)
    hc = min(chunk, HGRN_CHUNK)
    cfg = _Cfg(layer=layer, depth=depth, nb=nb, tt=tt, gc=chunk, hc=hc,
               has_s0=s0 is not None, emit_v=emit_v, final=final)
    rows = nb * tt

    def full(a):
        nd = a.ndim
        return pl.BlockSpec(a.shape, lambda b, t, _nd=nd: (0,) * _nd)

    tok_spec = pl.BlockSpec((nb, tt, d), lambda b, t: (b, t, 0))
    st_spec = pl.BlockSpec((nb, HGRN_HEADS, HGRN_DK, HGRN_DV), lambda b, t: (b, 0, 0, 0))
    ws = w["w_s"][:, :chunk, :chunk]
    bs_t = w["b_s"][:, :chunk].T
    params = [w["norm_g"], w["w_in"], w["ln_v_g"], w["ln_v_b"], ws, bs_t, w["lb_raw"], w["gnorm_g"],
              w["w_pa"], w["w_pb"], w["w_o"], w["final_g"]]
    args = [x, mod]
    in_specs = [tok_spec, pl.BlockSpec((nb, 3, d), lambda b, t: (b, 0, 0))]
    if s0 is not None:
        args.append(s0)
        in_specs.append(pl.BlockSpec((None, nb, HGRN_HEADS, HGRN_DK, HGRN_DV), lambda b, t: (layer, b, 0, 0, 0)))
    args += params
    in_specs += [full(a) for a in params]
    out_shape = [jax.ShapeDtypeStruct(x.shape, F32),
                 jax.ShapeDtypeStruct((batch, HGRN_HEADS, HGRN_DK, HGRN_DV), F32)]
    out_specs = [tok_spec, st_spec]
    if emit_v:
        out_shape.append(jax.ShapeDtypeStruct(x.shape, F32))
        out_specs.append(tok_spec)
    return pl.pallas_call(
        functools.partial(_layer_body, cfg),
        grid=(batch // nb, seq // tt),
        in_specs=in_specs,
        out_specs=out_specs,
        out_shape=out_shape,
        scratch_shapes=[pltpu.VMEM((nb, HGRN_HEADS, HGRN_DV, HGRN_DK), F32),
                        pltpu.VMEM((rows, d), BF16),
                        pltpu.VMEM((rows // hc, d // LANES, hc, LANES), F32),
                        pltpu.VMEM((rows // hc, hc, d), F32)],
        compiler_params=pltpu.CompilerParams(
            dimension_semantics=("arbitrary", "arbitrary"),
            vmem_limit_bytes=V7X_VMEM_LIMIT_BYTES),
        name=f"trunk_layer{layer}_{'sample' if emit_v else 'prompt'}",
    )(*args)


def _ada_body(c_ref, w_ref, b_ref, o_ref):
    c = c_ref[...]
    o_ref[0] = jnp.dot(_silu(c), w_ref[0], preferred_element_type=F32,
                       precision=lax.Precision.HIGHEST) + b_ref[0]


def _ada_modulation(c_all, w_ada, b_ada):
    depth, d, d3 = w_ada.shape
    n = c_all.shape[0]
    return pl.pallas_call(
        _ada_body,
        grid=(depth, d3 // d),
        in_specs=[pl.BlockSpec((n, d), lambda l, j: (0, 0)),
                  pl.BlockSpec((1, d, d), lambda l, j: (l, 0, j)),
                  pl.BlockSpec((1, 1, d), lambda l, j: (l, 0, j))],
        out_specs=pl.BlockSpec((1, n, d), lambda l, j: (l, 0, j)),
        out_shape=jax.ShapeDtypeStruct((depth, n, d3), F32),
        compiler_params=pltpu.CompilerParams(dimension_semantics=("arbitrary", "arbitrary")),
        name="ada_modulation",
    )(c_all, w_ada, b_ada.reshape(depth, 1, d3))


def kernel(x_prompt, x_sample, state_hgrn, c_prompt, c_sample, w_ada, b_ada, norm_g, w_in, ln_v_g, ln_v_b,
           w_s, b_s, lb_raw, gnorm_g, w_pa, w_pb, w_o, final_g):
    depth, d = norm_g.shape
    n_prompt = x_prompt.shape[0]
    mod = _ada_modulation(jnp.concatenate([c_prompt, c_sample], axis=0), w_ada, b_ada)
    mod = mod.reshape(depth, -1, 3, d)

    xp, xs = x_prompt, x_sample
    sp_list, ss_list, vs_list = [], [], []
    for l in range(depth):
        w = dict(norm_g=norm_g[l][None], w_in=_pack_rows_bf16(w_in, l), ln_v_g=ln_v_g[l][None],
                 ln_v_b=ln_v_b[l][None], w_s=w_s[l], b_s=b_s[l], lb_raw=lb_raw, gnorm_g=gnorm_g[l][None],
                 w_pa=_pack_rows_bf16(w_pa, l), w_pb=_pack_rows_bf16(w_pb, l), w_o=_pack_rows_bf16(w_o, l),
                 final_g=final_g[None])
        final = l == depth - 1
        xp, sp = _trunk_layer(l, depth, xp, mod[l, :n_prompt], None, w, final, False)
        xs, ss, vs = _trunk_layer(l, depth, xs, mod[l, n_prompt:], state_hgrn, w, final, True)
        sp_list.append(sp)
        ss_list.append(ss)
        vs_list.append(vs)
    return (xp, xs, jnp.stack(sp_list), jnp.stack(ss_list), jnp.stack(vs_list))
```

```python
import functools
import math
from typing import NamedTuple

import jax
import jax.numpy as jnp
from jax import lax
from jax.experimental import pallas as pl
from jax.experimental.pallas import tpu as pltpu

F32 = jnp.float32
BF16 = jnp.bfloat16

EPS = 1e-6
LANES = 128
GMLP_CHUNK = 128
GMLP_GROUPS = 4
HGRN_HEADS = 8
HGRN_DK = 128
HGRN_DV = 128
N_IN_SLABS = 9
SLAB_U, SLAB_V, SLAB_ZA, SLAB_Q, SLAB_F, SLAB_I, SLAB_ZB, SLAB_GA, SLAB_GB = range(N_IN_SLABS)

HGRN_CHUNK = 128
HGRN_BLOCKS = 4
SUBLANES = 8
EXP2_CLAMP = 115.0
LOG2E = math.log2(math.e)
TILE_ROWS = 256
PACK_BLOCK_COLS = 1024
V7X_VMEM_LIMIT_BYTES = 60000 * 1024

_NT = (((1,), (1,)), ((), ()))
_TN = (((0,), (0,)), ((), ()))


class _Cfg(NamedTuple):
    layer: int
    depth: int
    nb: int
    tt: int
    gc: int
    hc: int
    has_s0: bool
    emit_v: bool
    final: bool


def _gelu_tanh(x):
    c = math.sqrt(2.0 / math.pi)
    return 0.5 * x * (1.0 + jnp.tanh(c * (x + 0.044715 * (x * x * x))))


def _exp_neg(x):
    return jnp.exp2(x * (-LOG2E))


def _sigmoid(x):
    return 1.0 / (1.0 + _exp_neg(x))


def _silu(x):
    return x * _sigmoid(x)


def _rms(x):
    return lax.rsqrt(jnp.mean(x * x, axis=-1, keepdims=True) + EPS)


def _pack_body(w_ref, o_ref):
    o_ref[...] = pltpu.bitcast(w_ref[...].astype(BF16), jnp.uint32)


def _pack_rows_bf16(w_stack, layer):
    _, k, n = w_stack.shape
    bn = min(n, PACK_BLOCK_COLS)
    assert n % bn == 0
    return pl.pallas_call(
        _pack_body,
        grid=(n // bn,),
        in_specs=[pl.BlockSpec((None, k, bn), lambda j: (layer, 0, j))],
        out_specs=pl.BlockSpec((k // 2, bn), lambda j: (0, j)),
        out_shape=jax.ShapeDtypeStruct((k // 2, n), jnp.uint32),
        compiler_params=pltpu.CompilerParams(dimension_semantics=("arbitrary",)),
        name="pack_bf16_rows",
    )(w_stack)


def _as_bf16(packed):
    return pltpu.bitcast(packed, BF16)


def _row_bcast(a, idx, n_rep):
    return jnp.broadcast_to(a[idx:idx + 1, :], (n_rep, a.shape[1]))


def _forget_lower_bound(lbraw_ref, layer, depth):
    raw = lbraw_ref[...]
    mx = jnp.max(raw, axis=0, keepdims=True)
    ex = jnp.exp(raw - mx)
    den = jnp.sum(ex, axis=0, keepdims=True)
    num = jnp.zeros_like(den)
    for j in range(1, layer + 1):
        num = num + ex[j:j + 1, :]
    return num / den


def _hgrn_prepare(qs, kk, lf2, loc_ref):
    c, d = qs.shape
    nblk = HGRN_BLOCKS
    dg = c // nblk
    rep = SUBLANES // nblk
    rowi = lax.broadcasted_iota(jnp.int32, (c, d), 0)
    inblk = rowi & (dg - 1)
    loc = lf2
    sh = 1
    while sh < dg:
        loc = loc + jnp.where(inblk >= sh, pltpu.roll(loc, sh, 0), 0.0)
        sh *= 2
    n_lane_tiles = d // LANES
    for t in range(n_lane_tiles):
        loc_ref[t] = loc[:, t * LANES:(t + 1) * LANES]
    step = dg // rep
    part = jnp.concatenate([loc_ref[t, pl.ds(step - 1, SUBLANES, stride=step), :]
                            for t in range(n_lane_tiles)], axis=1)
    kidx = lax.broadcasted_iota(jnp.int32, (SUBLANES, d), 0)
    tot = part
    sh = 1
    while sh < rep:
        tot = jnp.where((kidx & (rep - 1)) >= rep - sh, tot, pltpu.roll(tot, SUBLANES - sh, 0))
        sh *= 2
    r_end = tot
    sh = rep
    while sh < SUBLANES:
        r_end = r_end + jnp.where(kidx >= sh, pltpu.roll(r_end, sh, 0), 0.0)
        sh *= 2
    r_start = r_end - tot
    b_last = r_end[SUBLANES - 1:SUBLANES, :]

    tot_rows = jnp.concatenate([_row_bcast(tot, j * rep, dg) for j in range(nblk)], axis=0)
    q_hat = qs * jnp.exp2(loc)
    k_hat = kk * jnp.exp2(tot_rows - loc)
    q_dg = q_hat.astype(BF16)
    k_dg = (kk * jnp.exp2(jnp.minimum(-loc, EXP2_CLAMP))).astype(BF16)

    def scaled_f32(fac, use_q):
        parts = [(q_hat if use_q[j] else k_hat)[j * dg:(j + 1) * dg] * _row_bcast(fac, j * rep, dg)
                 for j in range(nblk)]
        return jnp.concatenate(parts, axis=0)

    def scaled(fac, use_q):
        return scaled_f32(fac, use_q).astype(BF16)

    q_in = scaled(jnp.exp2(r_start), [True] * nblk)
    k_end_t = scaled_f32(jnp.exp2(b_last - r_end), [False] * nblk).T.astype(BF16)
    decay_t = jnp.broadcast_to(jnp.exp2(b_last), (HGRN_DV, d)).T

    ti = lax.broadcasted_iota(jnp.int32, (c, c), 0)
    si = lax.broadcasted_iota(jnp.int32, (c, c), 1)
    levels = []
    hb = nblk // 2
    while hb >= 1:
        m = None
        for g in reversed(range(nblk // (2 * hb))):
            row = _row_bcast(r_start, (g * 2 * hb + hb) * rep, SUBLANES)
            m = row if m is None else jnp.where(kidx < (g + 1) * 2 * hb * rep, row, m)
        fac = jnp.exp2(jnp.where((kidx & (hb * rep)) != 0, r_start - m, m - r_end))
        x = scaled(fac, [(j & hb) != 0 for j in range(nblk)])
        half = hb * dg
        mask = ((ti ^ si) < 2 * half) & ((ti & half) != 0) & ((si & half) == 0)
        levels.append((x, mask))
        hb //= 2
    worst = jnp.max(jnp.max(-tot, axis=1, keepdims=True), axis=0, keepdims=True)
    mask_dg = ((ti ^ si) < dg) & (si <= ti) & (worst <= EXP2_CLAMP)
    return dict(q_in=q_in, k_end_t=k_end_t, decay_t=decay_t, levels=levels, q_dg=q_dg, k_dg=k_dg,
                mask_dg=mask_dg, flagged=worst[0, 0] > EXP2_CLAMP, loc=loc, qs=qs, kk=kk)


def _hgrn_intra(p):
    out = []
    for h in range(HGRN_HEADS):
        hs = slice(h * HGRN_DK, (h + 1) * HGRN_DK)
        a = jnp.where(p["mask_dg"],
                      lax.dot_general(p["q_dg"][:, hs], p["k_dg"][:, hs], _NT, preferred_element_type=F32), 0.0)
        for x, mask in p["levels"]:
            xh = x[:, hs]
            a = jnp.where(mask, lax.dot_general(xh, xh, _NT, preferred_element_type=F32), a)
        out.append(a.astype(BF16))
    return out


def _hgrn_apply(p, a_heads, iv, st_ref, n, o_ref):
    c = iv.shape[0]
    for h in range(HGRN_HEADS):
        hs = slice(h * HGRN_DK, (h + 1) * HGRN_DK)
        vs = slice(h * HGRN_DV, (h + 1) * HGRN_DV)
        st = st_ref[n, h]
        stacked = jnp.dot(jnp.concatenate([a_heads[h], p["k_end_t"][hs, :]], axis=0), iv[:, vs],
                          preferred_element_type=F32)
        o_ref[:, vs] = jnp.dot(p["q_in"][:, hs], st.astype(BF16), preferred_element_type=F32) + stacked[:c]
        st_ref[n, h] = st * p["decay_t"][hs, :] + stacked[c:]


def _hgrn_exact_diag(p, iv):
    qs, kk, loc = p["qs"], p["kk"], p["loc"]
    c, d = qs.shape
    dg = c // HGRN_BLOCKS
    ivf = iv.astype(F32)
    inblk = lax.broadcasted_iota(jnp.int32, (c, d), 0) & (dg - 1)

    def body(dd, acc):
        valid = inblk >= dd
        expo = jnp.where(valid, loc - pltpu.roll(loc, dd, 0), 0.0)
        w = jnp.where(valid, qs * pltpu.roll(kk, dd, 0) * jnp.exp2(expo), 0.0)
        ir = pltpu.roll(ivf, dd, 0)
        parts = []
        for h in range(HGRN_HEADS):
            hs = slice(h * HGRN_DK, (h + 1) * HGRN_DK)
            vs = slice(h * HGRN_DV, (h + 1) * HGRN_DV)
            parts.append(jnp.sum(w[:, hs], axis=-1, keepdims=True) * ir[:, vs])
        return acc + jnp.concatenate(parts, axis=1)

    return lax.fori_loop(0, dg, body, jnp.zeros((c, d), F32))


def _layer_body(cfg, *refs):
    refs = list(refs)
    x_ref, mod_ref = refs[:2]
    pos = 2
    s0_ref = None
    if cfg.has_s0:
        s0_ref = refs[pos]
        pos += 1
    (normg_ref, win_ref, lnvg_ref, lnvb_ref, ws_ref, bs_ref, lbraw_ref, gng_ref,
     wpa_ref, wpb_ref, wo_ref, fing_ref) = refs[pos:pos + 12]
    pos += 12
    y_ref, sout_ref = refs[pos:pos + 2]
    pos += 2
    v_ref = None
    if cfg.emit_v:
        v_ref = refs[pos]
        pos += 1
    st_ref, yb_ref, loc_ref, o_ref = refs[pos:pos + 4]

    nb, tt = cfg.nb, cfg.tt
    rows = nb * tt
    d = x_ref.shape[-1]
    t_idx = pl.program_id(1)
    n_t = pl.num_programs(1)

    @pl.when(t_idx == 0)
    def _():
        for n in range(nb):
            for hd in range(HGRN_HEADS):
                if cfg.has_s0:
                    st_ref[n, hd] = s0_ref[n, hd]
                else:
                    st_ref[n, hd] = jnp.zeros((HGRN_DK, HGRN_DV), F32)

    hs = []
    for n in range(nb):
        x = x_ref[n]
        shift = mod_ref[n, 0:1, :]
        scale = mod_ref[n, 1:2, :]
        hs.append(((x * _rms(x)) * (normg_ref[...] * (1.0 + scale)) + shift).astype(BF16))
    h = hs[0] if nb == 1 else jnp.concatenate(hs, axis=0)

    def proj(slab):
        return jnp.dot(h, _as_bf16(win_ref[:, slab * d:(slab + 1) * d]), preferred_element_type=F32)

    z = proj(SLAB_F)
    pq = proj(SLAB_Q)
    iv = proj(SLAB_I).astype(BF16)
    pv = proj(SLAB_V)
    pu = proj(SLAB_U)
    pza = proj(SLAB_ZA)

    e = _exp_neg(jnp.abs(z))
    inv = 1.0 / (1.0 + e)
    log2_sig = jnp.minimum(z, 0.0) * LOG2E - jnp.log2(1.0 + e)
    sig_neg = jnp.where(z >= 0.0, e, 1.0) * inv
    if cfg.layer == 0:
        lf2 = log2_sig
        kk = sig_neg
    else:
        lb = _forget_lower_bound(lbraw_ref, cfg.layer, cfg.depth)
        f = lb + (1.0 - lb) * (jnp.where(z >= 0.0, 1.0, e) * inv)
        lf2 = jnp.where(f > 0.0, jnp.log2(f), log2_sig)
        kk = (1.0 - lb) * sig_neg
    qs = _silu(pq)

    v = _gelu_tanh(pv)
    mu = jnp.mean(v, axis=-1, keepdims=True)
    vc = v - mu
    var = jnp.mean(vc * vc, axis=-1, keepdims=True)
    vn = vc * lax.rsqrt(var + EPS) * lnvg_ref[...] + lnvb_ref[...]
    if cfg.emit_v:
        for n in range(nb):
            v_ref[n] = vn[n * tt:(n + 1) * tt]
    vb = vn.astype(BF16)
    gc = cfg.gc
    gdim = d // GMLP_GROUPS
    tril = (lax.broadcasted_iota(jnp.int32, (gc, gc), 0) >= lax.broadcasted_iota(jnp.int32, (gc, gc), 1))
    w_tril = [jnp.where(tril, ws_ref[g], 0.0).astype(BF16) for g in range(GMLP_GROUPS)]
    s_chunks = []
    for j in range(rows // gc):
        parts = [jnp.dot(w_tril[g], vb[j * gc:(j + 1) * gc, g * gdim:(g + 1) * gdim],
                         preferred_element_type=F32) + bs_ref[:, g:g + 1]
                 for g in range(GMLP_GROUPS)]
        s_chunks.append(jnp.concatenate(parts, axis=1))
    s = s_chunks[0] if len(s_chunks) == 1 else jnp.concatenate(s_chunks, axis=0)

    pzb = proj(SLAB_ZB)
    pga = proj(SLAB_GA)

    hc = cfg.hc
    cps = tt // hc
    n_chunks = rows // hc
    prep = [_hgrn_prepare(qs[j * hc:(j + 1) * hc], kk[j * hc:(j + 1) * hc], lf2[j * hc:(j + 1) * hc],
                          loc_ref.at[j])
            for j in range(n_chunks)]
    a_heads = [_hgrn_intra(p) for p in prep]

    ya = (_gelu_tanh(pu) * s * _silu(pza)).astype(BF16)
    m_a = _sigmoid(pga) * jnp.dot(ya, _as_bf16(wpa_ref[...]), preferred_element_type=F32)

    gate_b = _silu(pzb)
    gng = gng_ref[...]
    for j in range(n_chunks):
        _hgrn_apply(prep[j], a_heads[j], iv[j * hc:(j + 1) * hc], st_ref, j // cps, o_ref.at[j])
    sig_gb = _sigmoid(proj(SLAB_GB))

    def finish():
        for j in range(n_chunks):
            for hd in range(HGRN_HEADS):
                vs = slice(hd * HGRN_DV, (hd + 1) * HGRN_DV)
                o = o_ref[j, :, vs]
                yb_ref[j * hc:(j + 1) * hc, vs] = (o * _rms(o) * gng[:, vs]
                                                   * gate_b[j * hc:(j + 1) * hc, vs]).astype(BF16)
        m_b = sig_gb * jnp.dot(yb_ref[...], _as_bf16(wpb_ref[...]), preferred_element_type=F32)
        upd = jnp.dot((m_a + m_b).astype(BF16), _as_bf16(wo_ref[...]), preferred_element_type=F32)
        for n in range(nb):
            gate = mod_ref[n, 2:3, :]
            xo = x_ref[n] + gate * upd[n * tt:(n + 1) * tt]
            if cfg.final:
                xo = (xo * _rms(xo)) * fing_ref[...]
            y_ref[n] = xo

    finish()

    any_flagged = prep[0]["flagged"]
    for p in prep[1:]:
        any_flagged = jnp.logical_or(any_flagged, p["flagged"])

    @pl.when(any_flagged)
    def _():
        for j in range(n_chunks):
            @pl.when(prep[j]["flagged"])
            def _():
                o_ref[j] = o_ref[j] + _hgrn_exact_diag(prep[j], iv[j * hc:(j + 1) * hc])
        finish()

    @pl.when(t_idx == n_t - 1)
    def _():
        for n in range(nb):
            for hd in range(HGRN_HEADS):
                sout_ref[n, hd] = st_ref[n, hd]


def _tiling(batch, seq):
    chunk = min(seq, GMLP_CHUNK)
    assert seq % chunk == 0 and chunk % (8 * HGRN_BLOCKS) == 0
    if seq >= TILE_ROWS:
        assert seq % TILE_ROWS == 0
        return 1, TILE_ROWS, chunk
    nb = max(1, TILE_ROWS // seq)
    while batch % nb:
        nb -= 1
    return nb, seq, chunk


def _trunk_layer(layer, depth, x, mod, s0, w, final, emit_v):
    batch, seq, d = x.shape
    nb, tt, chunk = _tiling(batch, seq)
    hc = min(chunk, HGRN_CHUNK)
    n_b, n_t = batch // nb, seq // tt
    cfg = _Cfg(layer=layer, depth=depth, nb=nb, tt=tt, gc=chunk, hc=hc,
               has_s0=s0 is not None, emit_v=emit_v, final=final)
    rows = nb * tt

    def full(a):
        nd = a.ndim
        return pl.BlockSpec(a.shape, lambda b, t, _nd=nd: (0,) * _nd)

    tok_spec = pl.BlockSpec((nb, tt, d), lambda b, t: (b, t, 0))
    st_spec = pl.BlockSpec((nb, HGRN_HEADS, HGRN_DK, HGRN_DV), lambda b, t: (b, 0, 0, 0))
    ws = w["w_s"][:, :chunk, :chunk]
    bs_t = w["b_s"][:, :chunk].T
    params = [w["norm_g"], w["w_in"], w["ln_v_g"], w["ln_v_b"], ws, bs_t, w["lb_raw"], w["gnorm_g"],
              w["w_pa"], w["w_pb"], w["w_o"], w["final_g"]]

    args = [x, mod]
    in_specs = [tok_spec, pl.BlockSpec((nb, 3, d), lambda b, t: (b, 0, 0))]
    if s0 is not None:
        args.append(s0)
        in_specs.append(pl.BlockSpec((None, nb, HGRN_HEADS, HGRN_DK, HGRN_DV), lambda b, t: (layer, b, 0, 0, 0)))
    args += params
    in_specs += [full(a) for a in params]
    out_shape = [jax.ShapeDtypeStruct(x.shape, F32),
                 jax.ShapeDtypeStruct((batch, HGRN_HEADS, HGRN_DK, HGRN_DV), F32)]
    out_specs = [tok_spec, st_spec]
    if emit_v:
        out_shape.append(jax.ShapeDtypeStruct(x.shape, F32))
        out_specs.append(tok_spec)
    return pl.pallas_call(
        functools.partial(_layer_body, cfg),
        grid=(n_b, n_t),
        in_specs=in_specs,
        out_specs=out_specs,
        out_shape=out_shape,
        scratch_shapes=[pltpu.VMEM((nb, HGRN_HEADS, HGRN_DK, HGRN_DV), F32),
                        pltpu.VMEM((rows, d), BF16),
                        pltpu.VMEM((rows // hc, d // LANES, hc, LANES), F32),
                        pltpu.VMEM((rows // hc, hc, d), F32)],
        compiler_params=pltpu.CompilerParams(
            dimension_semantics=("arbitrary", "arbitrary"),
            vmem_limit_bytes=V7X_VMEM_LIMIT_BYTES),
        name=f"trunk_layer{layer}_{'sample' if emit_v else 'prompt'}",
    )(*args)


def _ada_body(c_ref, w_ref, b_ref, o_ref):
    c = c_ref[...]
    o_ref[0] = jnp.dot(_silu(c), w_ref[0], preferred_element_type=F32,
                       precision=lax.Precision.HIGHEST) + b_ref[0]


def _ada_modulation(c_all, w_ada, b_ada):
    depth, d, d3 = w_ada.shape
    n = c_all.shape[0]
    return pl.pallas_call(
        _ada_body,
        grid=(depth, d3 // d),
        in_specs=[pl.BlockSpec((n, d), lambda l, j: (0, 0)),
                  pl.BlockSpec((1, d, d), lambda l, j: (l, 0, j)),
                  pl.BlockSpec((1, 1, d), lambda l, j: (l, 0, j))],
        out_specs=pl.BlockSpec((1, n, d), lambda l, j: (l, 0, j)),
        out_shape=jax.ShapeDtypeStruct((depth, n, d3), F32),
        compiler_params=pltpu.CompilerParams(dimension_semantics=("arbitrary", "arbitrary")),
        name="ada_modulation",
    )(c_all, w_ada, b_ada.reshape(depth, 1, d3))


def kernel(x_prompt, x_sample, state_hgrn, c_prompt, c_sample, w_ada, b_ada, norm_g, w_in, ln_v_g, ln_v_b,
           w_s, b_s, lb_raw, gnorm_g, w_pa, w_pb, w_o, final_g):
    depth, d = norm_g.shape
    n_prompt = x_prompt.shape[0]
    mod = _ada_modulation(jnp.concatenate([c_prompt, c_sample], axis=0), w_ada, b_ada)
    mod = mod.reshape(depth, -1, 3, d)

    xp, xs = x_prompt, x_sample
    sp_list, ss_list, vs_list = [], [], []
    for l in range(depth):
        w = dict(norm_g=norm_g[l][None], w_in=_pack_rows_bf16(w_in, l), ln_v_g=ln_v_g[l][None],
                 ln_v_b=ln_v_b[l][None], w_s=w_s[l], b_s=b_s[l], lb_raw=lb_raw, gnorm_g=gnorm_g[l][None],
                 w_pa=_pack_rows_bf16(w_pa, l), w_pb=_pack_rows_bf16(w_pb, l), w_o=_pack_rows_bf16(w_o, l),
                 final_g=final_g[None])
        final = l == depth - 1
        xp, sp = _trunk_layer(l, depth, xp, mod[l, :n_prompt], None, w, final, False)
        xs, ss, vs = _trunk_layer(l, depth, xs, mod[l, n_prompt:], state_hgrn, w, final, True)
        sp_list.append(sp)
        ss_list.append(ss)
        vs_list.append(vs)
    return (xp, xs, jnp.stack(sp_list), jnp.stack(ss_list), jnp.stack(vs_list))
```

```python
import functools
import math
from typing import NamedTuple

import jax
import jax.numpy as jnp
from jax import lax
from jax.experimental import pallas as pl
from jax.experimental.pallas import tpu as pltpu

F32 = jnp.float32
BF16 = jnp.bfloat16

EPS = 1e-6
LANES = 128
GMLP_CHUNK = 128
GMLP_GROUPS = 4
HGRN_HEADS = 8
HGRN_DK = 128
HGRN_DV = 128
N_IN_SLABS = 9
SLAB_U, SLAB_V, SLAB_ZA, SLAB_Q, SLAB_F, SLAB_I, SLAB_ZB, SLAB_GA, SLAB_GB = range(N_IN_SLABS)

HGRN_CHUNK = 128
HGRN_BLOCKS = 4
SUBLANES = 8
EXP2_CLAMP = 115.0
LOG2E = math.log2(math.e)
TILE_ROWS = 256
PACK_BLOCK_COLS = 1024
V7X_VMEM_LIMIT_BYTES = 60000 * 1024

_NT = (((1,), (1,)), ((), ()))
_TN = (((0,), (0,)), ((), ()))


class _Cfg(NamedTuple):
    layer: int
    depth: int
    nb: int
    tt: int
    gc: int
    hc: int
    has_s0: bool
    emit_v: bool
    final: bool


def _gelu_tanh(x):
    c = math.sqrt(2.0 / math.pi)
    half_x = 0.5 * x
    return half_x + half_x * jnp.tanh(x * (c + (c * 0.044715) * (x * x)))


def _exp_neg(x):
    return jnp.exp2(x * (-LOG2E))


def _sigmoid(x):
    return 1.0 / (1.0 + _exp_neg(x))


def _silu(x):
    return x * _sigmoid(x)


def _rms(x):
    return lax.rsqrt(jnp.mean(x * x, axis=-1, keepdims=True) + EPS)


def _pack_body(w_ref, o_ref):
    o_ref[...] = pltpu.bitcast(w_ref[...].astype(BF16), jnp.uint32)


def _pack_rows_bf16(w_stack, layer):
    _, k, n = w_stack.shape
    bn = min(n, PACK_BLOCK_COLS)
    assert n % bn == 0
    return pl.pallas_call(
        _pack_body,
        grid=(n // bn,),
        in_specs=[pl.BlockSpec((None, k, bn), lambda j: (layer, 0, j))],
        out_specs=pl.BlockSpec((k // 2, bn), lambda j: (0, j)),
        out_shape=jax.ShapeDtypeStruct((k // 2, n), jnp.uint32),
        compiler_params=pltpu.CompilerParams(dimension_semantics=("arbitrary",)),
        name="pack_bf16_rows",
    )(w_stack)


def _as_bf16(packed):
    return pltpu.bitcast(packed, BF16)


def _row_bcast(a, idx, n_rep):
    return jnp.broadcast_to(a[idx:idx + 1, :], (n_rep, a.shape[1]))


def _forget_lower_bound(lbraw_ref, layer, depth):
    raw = lbraw_ref[...]
    mx = jnp.max(raw, axis=0, keepdims=True)
    ex = jnp.exp(raw - mx)
    den = jnp.sum(ex, axis=0, keepdims=True)
    num = jnp.zeros_like(den)
    for j in range(1, layer + 1):
        num = num + ex[j:j + 1, :]
    return num / den


def _hgrn_prepare(qs, kk, lf2, loc_ref):
    c, d = qs.shape
    nblk = HGRN_BLOCKS
    dg = c // nblk
    rep = SUBLANES // nblk
    kidx = lax.broadcasted_iota(jnp.int32, (SUBLANES, d), 0)
    slabs = []
    for j in range(c // SUBLANES):
        v = lf2[j * SUBLANES:(j + 1) * SUBLANES]
        sh = 1
        while sh < SUBLANES:
            v = v + jnp.where(kidx >= sh, pltpu.roll(v, sh, 0), 0.0)
            sh *= 2
        if j % (dg // SUBLANES):
            v = v + _row_bcast(slabs[-1], SUBLANES - 1, SUBLANES)
        slabs.append(v)
    loc = jnp.concatenate(slabs, axis=0)
    n_lane_tiles = d // LANES
    for t in range(n_lane_tiles):
        loc_ref[t] = loc[:, t * LANES:(t + 1) * LANES]
    step = dg // rep
    part = jnp.concatenate([loc_ref[t, pl.ds(step - 1, SUBLANES, stride=step), :]
                            for t in range(n_lane_tiles)], axis=1)
    tot = part
    sh = 1
    while sh < rep:
        tot = jnp.where((kidx & (rep - 1)) >= rep - sh, tot, pltpu.roll(tot, SUBLANES - sh, 0))
        sh *= 2
    r_end = tot
    sh = rep
    while sh < SUBLANES:
        r_end = r_end + jnp.where(kidx >= sh, pltpu.roll(r_end, sh, 0), 0.0)
        sh *= 2
    r_start = r_end - tot
    b_last = r_end[SUBLANES - 1:SUBLANES, :]

    tot_rows = jnp.concatenate([_row_bcast(tot, j * rep, dg) for j in range(nblk)], axis=0)
    q_hat = qs * jnp.exp2(loc)
    k_hat = kk * jnp.exp2(tot_rows - loc)
    q_dg = q_hat.astype(BF16)
    k_dg = (kk * jnp.exp2(jnp.minimum(-loc, EXP2_CLAMP))).astype(BF16)

    def scaled_f32(fac, use_q):
        parts = [(q_hat if use_q[j] else k_hat)[j * dg:(j + 1) * dg] * _row_bcast(fac, j * rep, dg)
                 for j in range(nblk)]
        return jnp.concatenate(parts, axis=0)

    def scaled(fac, use_q):
        return scaled_f32(fac, use_q).astype(BF16)

    q_in = scaled(jnp.exp2(r_start), [True] * nblk)
    k_end_t = scaled_f32(jnp.exp2(b_last - r_end), [False] * nblk).T.astype(BF16)
    decay_t = jnp.broadcast_to(jnp.exp2(b_last), (HGRN_DV, d)).T

    ti = lax.broadcasted_iota(jnp.int32, (c, c), 0)
    si = lax.broadcasted_iota(jnp.int32, (c, c), 1)
    levels = []
    hb = nblk // 2
    while hb >= 1:
        m = None
        for g in reversed(range(nblk // (2 * hb))):
            row = _row_bcast(r_start, (g * 2 * hb + hb) * rep, SUBLANES)
            m = row if m is None else jnp.where(kidx < (g + 1) * 2 * hb * rep, row, m)
        fac = jnp.exp2(jnp.where((kidx & (hb * rep)) != 0, r_start - m, m - r_end))
        x = scaled(fac, [(j & hb) != 0 for j in range(nblk)])
        half = hb * dg
        mask = ((ti ^ si) < 2 * half) & ((ti & half) != 0) & ((si & half) == 0)
        levels.append((x, mask))
        hb //= 2
    worst = jnp.max(jnp.max(-tot, axis=1, keepdims=True), axis=0, keepdims=True)
    mask_dg = ((ti ^ si) < dg) & (si <= ti) & (worst <= EXP2_CLAMP)
    return dict(q_in=q_in, k_end_t=k_end_t, decay_t=decay_t, levels=levels, q_dg=q_dg, k_dg=k_dg,
                mask_dg=mask_dg, flagged=worst[0, 0] > EXP2_CLAMP, loc=loc, qs=qs, kk=kk)


def _hgrn_intra(p):
    out = []
    for h in range(HGRN_HEADS):
        hs = slice(h * HGRN_DK, (h + 1) * HGRN_DK)
        a = jnp.where(p["mask_dg"],
                      lax.dot_general(p["q_dg"][:, hs], p["k_dg"][:, hs], _NT, preferred_element_type=F32), 0.0)
        for x, mask in p["levels"]:
            xh = x[:, hs]
            a = jnp.where(mask, lax.dot_general(xh, xh, _NT, preferred_element_type=F32), a)
        out.append(a.astype(BF16))
    return out


def _hgrn_apply(p, a_heads, iv, st_ref, n, o_ref):
    c = iv.shape[0]
    for h in range(HGRN_HEADS):
        hs = slice(h * HGRN_DK, (h + 1) * HGRN_DK)
        vs = slice(h * HGRN_DV, (h + 1) * HGRN_DV)
        st = st_ref[n, h]
        stacked = jnp.dot(jnp.concatenate([a_heads[h], p["k_end_t"][hs, :]], axis=0), iv[:, vs],
                          preferred_element_type=F32)
        o_ref[:, vs] = jnp.dot(p["q_in"][:, hs], st.astype(BF16), preferred_element_type=F32) + stacked[:c]
        st_ref[n, h] = st * p["decay_t"][hs, :] + stacked[c:]


def _hgrn_exact_diag(p, iv):
    qs, kk, loc = p["qs"], p["kk"], p["loc"]
    c, d = qs.shape
    dg = c // HGRN_BLOCKS
    ivf = iv.astype(F32)
    inblk = lax.broadcasted_iota(jnp.int32, (c, d), 0) & (dg - 1)

    def body(dd, acc):
        valid = inblk >= dd
        expo = jnp.where(valid, loc - pltpu.roll(loc, dd, 0), 0.0)
        w = jnp.where(valid, qs * pltpu.roll(kk, dd, 0) * jnp.exp2(expo), 0.0)
        ir = pltpu.roll(ivf, dd, 0)
        parts = []
        for h in range(HGRN_HEADS):
            hs = slice(h * HGRN_DK, (h + 1) * HGRN_DK)
            vs = slice(h * HGRN_DV, (h + 1) * HGRN_DV)
            parts.append(jnp.sum(w[:, hs], axis=-1, keepdims=True) * ir[:, vs])
        return acc + jnp.concatenate(parts, axis=1)

    return lax.fori_loop(0, dg, body, jnp.zeros((c, d), F32))


def _layer_body(cfg, *refs):
    refs = list(refs)
    x_ref, mod_ref = refs[:2]
    pos = 2
    s0_ref = None
    if cfg.has_s0:
        s0_ref = refs[pos]
        pos += 1
    (normg_ref, win_ref, lnvg_ref, lnvb_ref, ws_ref, bs_ref, lbraw_ref, gng_ref,
     wpa_ref, wpb_ref, wo_ref, fing_ref) = refs[pos:pos + 12]
    pos += 12
    pos += 2 if cfg.emit_v else 1
    y_ref, sout_ref = refs[pos:pos + 2]
    pos += 2
    v_ref = None
    if cfg.emit_v:
        v_ref = refs[pos]
        pos += 1
    st_ref, yb_ref, loc_ref, o_ref = refs[pos:pos + 4]

    nb, tt = cfg.nb, cfg.tt
    rows = nb * tt
    d = x_ref.shape[-1]
    t_idx = pl.program_id(1)
    n_t = pl.num_programs(1)

    @pl.when(t_idx == 0)
    def _():
        for n in range(nb):
            for hd in range(HGRN_HEADS):
                if cfg.has_s0:
                    st_ref[n, hd] = s0_ref[n, hd]
                else:
                    st_ref[n, hd] = jnp.zeros((HGRN_DK, HGRN_DV), F32)

    hs = []
    for n in range(nb):
        x = x_ref[n]
        shift = mod_ref[n, 0:1, :]
        scale = mod_ref[n, 1:2, :]
        hs.append(((x * _rms(x)) * (normg_ref[...] * (1.0 + scale)) + shift).astype(BF16))
    h = hs[0] if nb == 1 else jnp.concatenate(hs, axis=0)

    def proj(slab):
        return jnp.dot(h, _as_bf16(win_ref[:, slab * d:(slab + 1) * d]), preferred_element_type=F32)

    z = proj(SLAB_F)
    pq = proj(SLAB_Q)
    iv = proj(SLAB_I).astype(BF16)
    pv = proj(SLAB_V)
    pu = proj(SLAB_U)
    pza = proj(SLAB_ZA)

    e = _exp_neg(jnp.abs(z))
    inv = 1.0 / (1.0 + e)
    log2_sig = jnp.minimum(z, 0.0) * LOG2E - jnp.log2(1.0 + e)
    sig_neg = jnp.where(z >= 0.0, e, 1.0) * inv
    if cfg.layer == 0:
        lf2 = log2_sig
        kk = sig_neg
    else:
        lb = _forget_lower_bound(lbraw_ref, cfg.layer, cfg.depth)
        f = lb + (1.0 - lb) * (jnp.where(z >= 0.0, 1.0, e) * inv)
        lf2 = jnp.where(f > 0.0, jnp.log2(f), log2_sig)
        kk = (1.0 - lb) * sig_neg
    qs = _silu(pq)

    v = _gelu_tanh(pv)
    mu = jnp.mean(v, axis=-1, keepdims=True)
    vc = v - mu
    var = jnp.mean(vc * vc, axis=-1, keepdims=True)
    vn = vc * lax.rsqrt(var + EPS) * lnvg_ref[...] + lnvb_ref[...]
    if cfg.emit_v:
        for n in range(nb):
            v_ref[n] = vn[n * tt:(n + 1) * tt]
    vb = vn.astype(BF16)
    gc = cfg.gc
    gdim = d // GMLP_GROUPS
    tril = (lax.broadcasted_iota(jnp.int32, (gc, gc), 0) >= lax.broadcasted_iota(jnp.int32, (gc, gc), 1))
    w_tril = [jnp.where(tril, ws_ref[g], 0.0).astype(BF16) for g in range(GMLP_GROUPS)]
    s_chunks = []
    for j in range(rows // gc):
        parts = [jnp.dot(w_tril[g], vb[j * gc:(j + 1) * gc, g * gdim:(g + 1) * gdim],
                         preferred_element_type=F32) + bs_ref[:, g:g + 1]
                 for g in range(GMLP_GROUPS)]
        s_chunks.append(jnp.concatenate(parts, axis=1))
    s = s_chunks[0] if len(s_chunks) == 1 else jnp.concatenate(s_chunks, axis=0)

    pzb = proj(SLAB_ZB)
    pga = proj(SLAB_GA)

    hc = cfg.hc
    cps = tt // hc
    n_chunks = rows // hc
    prep = [_hgrn_prepare(qs[j * hc:(j + 1) * hc], kk[j * hc:(j + 1) * hc], lf2[j * hc:(j + 1) * hc],
                          loc_ref.at[j])
            for j in range(n_chunks)]
    a_heads = [_hgrn_intra(p) for p in prep]

    ya = (_gelu_tanh(pu) * s * _silu(pza)).astype(BF16)
    m_a = _sigmoid(pga) * jnp.dot(ya, _as_bf16(wpa_ref[...]), preferred_element_type=F32)

    gate_b = _silu(pzb)
    gng = gng_ref[...]
    for j in range(n_chunks):
        _hgrn_apply(prep[j], a_heads[j], iv[j * hc:(j + 1) * hc], st_ref, j // cps, o_ref.at[j])
    sig_gb = _sigmoid(proj(SLAB_GB))

    def finish():
        for j in range(n_chunks):
            for hd in range(HGRN_HEADS):
                vs = slice(hd * HGRN_DV, (hd + 1) * HGRN_DV)
                o = o_ref[j, :, vs]
                yb_ref[j * hc:(j + 1) * hc, vs] = (o * _rms(o) * gng[:, vs]
                                                   * gate_b[j * hc:(j + 1) * hc, vs]).astype(BF16)
        m_b = sig_gb * jnp.dot(yb_ref[...], _as_bf16(wpb_ref[...]), preferred_element_type=F32)
        upd = jnp.dot((m_a + m_b).astype(BF16), _as_bf16(wo_ref[...]), preferred_element_type=F32)
        for n in range(nb):
            gate = mod_ref[n, 2:3, :]
            xo = x_ref[n] + gate * upd[n * tt:(n + 1) * tt]
            if cfg.final:
                xo = (xo * _rms(xo)) * fing_ref[...]
            y_ref[n] = xo

    finish()

    any_flagged = prep[0]["flagged"]
    for p in prep[1:]:
        any_flagged = jnp.logical_or(any_flagged, p["flagged"])

    @pl.when(any_flagged)
    def _():
        for j in range(n_chunks):
            @pl.when(prep[j]["flagged"])
            def _():
                o_ref[j] = o_ref[j] + _hgrn_exact_diag(prep[j], iv[j * hc:(j + 1) * hc])
        finish()

    @pl.when(t_idx == n_t - 1)
    def _():
        for n in range(nb):
            for hd in range(HGRN_HEADS):
                sout_ref[n, hd] = st_ref[n, hd]


def _tiling(batch, seq):
    chunk = min(seq, GMLP_CHUNK)
    assert seq % chunk == 0 and chunk % (8 * HGRN_BLOCKS) == 0
    if seq >= TILE_ROWS:
        assert seq % TILE_ROWS == 0
        return 1, TILE_ROWS, chunk
    nb = max(1, TILE_ROWS // seq)
    while batch % nb:
        nb -= 1
    return nb, seq, chunk


def _trunk_layer(layer, depth, x, mod, s0, w, final, state_stack, v_stack):
    emit_v = v_stack is not None
    batch, seq, d = x.shape
    nb, tt, chunk = _tiling(batch, seq)
    hc = min(chunk, HGRN_CHUNK)
    n_b, n_t = batch // nb, seq // tt
    cfg = _Cfg(layer=layer, depth=depth, nb=nb, tt=tt, gc=chunk, hc=hc,
               has_s0=s0 is not None, emit_v=emit_v, final=final)
    rows = nb * tt

    def full(a):
        nd = a.ndim
        return pl.BlockSpec(a.shape, lambda b, t, _nd=nd: (0,) * _nd)

    tok_spec = pl.BlockSpec((nb, tt, d), lambda b, t: (b, t, 0))
    st_spec = pl.BlockSpec((None, nb, HGRN_HEADS, HGRN_DK, HGRN_DV), lambda b, t: (layer, b, 0, 0, 0))
    v_spec = pl.BlockSpec((None, nb, tt, d), lambda b, t: (layer, b, t, 0))
    ws = w["w_s"][:, :chunk, :chunk]
    bs_t = w["b_s"][:, :chunk].T
    params = [w["norm_g"], w["w_in"], w["ln_v_g"], w["ln_v_b"], ws, bs_t, w["lb_raw"], w["gnorm_g"],
              w["w_pa"], w["w_pb"], w["w_o"], w["final_g"]]

    args = [x, mod]
    in_specs = [tok_spec, pl.BlockSpec((nb, 3, d), lambda b, t: (b, 0, 0))]
    if s0 is not None:
        args.append(s0)
        in_specs.append(st_spec)
    args += params
    in_specs += [full(a) for a in params]
    stacks = [state_stack] + ([v_stack] if emit_v else [])
    aliases = {len(args) + i: 1 + i for i in range(len(stacks))}
    args += stacks
    in_specs += [pl.BlockSpec(memory_space=pl.ANY)] * len(stacks)
    out_shape = [jax.ShapeDtypeStruct(x.shape, F32)] + [jax.ShapeDtypeStruct(a.shape, F32) for a in stacks]
    out_specs = [tok_spec, st_spec] + ([v_spec] if emit_v else [])
    return pl.pallas_call(
        functools.partial(_layer_body, cfg),
        grid=(n_b, n_t),
        in_specs=in_specs,
        out_specs=out_specs,
        out_shape=out_shape,
        input_output_aliases=aliases,
        scratch_shapes=[pltpu.VMEM((nb, HGRN_HEADS, HGRN_DK, HGRN_DV), F32),
                        pltpu.VMEM((rows, d), BF16),
                        pltpu.VMEM((rows // hc, d // LANES, hc, LANES), F32),
                        pltpu.VMEM((rows // hc, hc, d), F32)],
        compiler_params=pltpu.CompilerParams(
            dimension_semantics=("arbitrary", "arbitrary"),
            vmem_limit_bytes=V7X_VMEM_LIMIT_BYTES),
        name=f"trunk_layer{layer}_{'sample' if emit_v else 'prompt'}",
    )(*args)


def _ada_body(c_ref, w_ref, b_ref, o_ref):
    c = c_ref[...]
    o_ref[0] = jnp.dot(_silu(c), w_ref[0], preferred_element_type=F32,
                       precision=lax.Precision.HIGHEST) + b_ref[0]


def _ada_modulation(c_all, w_ada, b_ada):
    depth, d, d3 = w_ada.shape
    n = c_all.shape[0]
    return pl.pallas_call(
        _ada_body,
        grid=(depth, d3 // d),
        in_specs=[pl.BlockSpec((n, d), lambda l, j: (0, 0)),
                  pl.BlockSpec((1, d, d), lambda l, j: (l, 0, j)),
                  pl.BlockSpec((1, 1, d), lambda l, j: (l, 0, j))],
        out_specs=pl.BlockSpec((1, n, d), lambda l, j: (l, 0, j)),
        out_shape=jax.ShapeDtypeStruct((depth, n, d3), F32),
        compiler_params=pltpu.CompilerParams(dimension_semantics=("arbitrary", "arbitrary")),
        name="ada_modulation",
    )(c_all, w_ada, b_ada.reshape(depth, 1, d3))


def kernel(x_prompt, x_sample, state_hgrn, c_prompt, c_sample, w_ada, b_ada, norm_g, w_in, ln_v_g, ln_v_b,
           w_s, b_s, lb_raw, gnorm_g, w_pa, w_pb, w_o, final_g):
    depth, d = norm_g.shape
    n_prompt = x_prompt.shape[0]
    mod = _ada_modulation(jnp.concatenate([c_prompt, c_sample], axis=0), w_ada, b_ada)
    mod = mod.reshape(depth, -1, 3, d)

    xp, xs = x_prompt, x_sample
    n_sample = x_sample.shape[0]
    sp = jnp.zeros((depth, n_prompt, HGRN_HEADS, HGRN_DK, HGRN_DV), F32)
    ss = jnp.zeros((depth, n_sample, HGRN_HEADS, HGRN_DK, HGRN_DV), F32)
    vs = jnp.zeros((depth,) + x_sample.shape, F32)
    for l in range(depth):
        w = dict(norm_g=norm_g[l][None], w_in=_pack_rows_bf16(w_in, l), ln_v_g=ln_v_g[l][None],
                 ln_v_b=ln_v_b[l][None], w_s=w_s[l], b_s=b_s[l], lb_raw=lb_raw, gnorm_g=gnorm_g[l][None],
                 w_pa=_pack_rows_bf16(w_pa, l), w_pb=_pack_rows_bf16(w_pb, l), w_o=_pack_rows_bf16(w_o, l),
                 final_g=final_g[None])
        final = l == depth - 1
        xp, sp = _trunk_layer(l, depth, xp, mod[l, :n_prompt], None, w, final, sp, None)
        xs, ss, vs = _trunk_layer(l, depth, xs, mod[l, n_prompt:], state_hgrn, w, final, ss, vs)
    return (xp, xs, sp, ss, vs)
```

```python
import functools
import math
from typing import NamedTuple

import jax
import jax.numpy as jnp
from jax import lax
from jax.experimental import pallas as pl
from jax.experimental.pallas import tpu as pltpu

F32 = jnp.float32
BF16 = jnp.bfloat16

EPS = 1e-6
LANES = 128
GMLP_CHUNK = 128
GMLP_GROUPS = 4
HGRN_HEADS = 8
HGRN_DK = 128
HGRN_DV = 128
N_IN_SLABS = 9
SLAB_U, SLAB_V, SLAB_ZA, SLAB_Q, SLAB_F, SLAB_I, SLAB_ZB, SLAB_GA, SLAB_GB = range(N_IN_SLABS)

HGRN_CHUNK = 128
HGRN_SUBBLOCK = 32
SUBLANES = 8
EXP2_CLAMP = 115.0
LOG2E = math.log2(math.e)
TILE_ROWS = 256
PACK_BLOCK_COLS = 1024
V7X_VMEM_LIMIT_BYTES = 60000 * 1024

_NT = (((1,), (1,)), ((), ()))


class _Cfg(NamedTuple):
    layer: int
    nb: int
    tt: int
    gc: int
    hc: int
    has_s0: bool
    emit_v: bool
    final: bool


def _gelu_tanh(x):
    c = math.sqrt(2.0 / math.pi)
    half_x = 0.5 * x
    return half_x + half_x * jnp.tanh(x * (c + (c * 0.044715) * (x * x)))


def _exp_neg(x):
    return jnp.exp2(x * (-LOG2E))


def _sigmoid(x):
    return 1.0 / (1.0 + _exp_neg(x))


def _silu(x):
    return x * _sigmoid(x)


def _rms(x):
    return lax.rsqrt(jnp.mean(x * x, axis=-1, keepdims=True) + EPS)


def _pack_body(w_ref, o_ref):
    o_ref[...] = pltpu.bitcast(w_ref[...].astype(BF16), jnp.uint32)


def _pack_rows_bf16(w_stack, layer):
    _, k, n = w_stack.shape
    bn = min(n, PACK_BLOCK_COLS)
    assert n % bn == 0
    return pl.pallas_call(
        _pack_body,
        grid=(n // bn,),
        in_specs=[pl.BlockSpec((None, k, bn), lambda j: (layer, 0, j))],
        out_specs=pl.BlockSpec((k // 2, bn), lambda j: (0, j)),
        out_shape=jax.ShapeDtypeStruct((k // 2, n), jnp.uint32),
        compiler_params=pltpu.CompilerParams(dimension_semantics=("arbitrary",)),
        name="pack_bf16_rows",
    )(w_stack)


def _as_bf16(packed):
    return pltpu.bitcast(packed, BF16)


def _row_bcast(a, idx, n_rep):
    return jnp.broadcast_to(a[idx:idx + 1, :], (n_rep, a.shape[1]))


def _forget_lower_bound(lbraw_ref, layer):
    raw = lbraw_ref[...]
    mx = jnp.max(raw, axis=0, keepdims=True)
    ex = jnp.exp(raw - mx)
    den = jnp.sum(ex, axis=0, keepdims=True)
    num = jnp.zeros_like(den)
    for j in range(1, layer + 1):
        num = num + ex[j:j + 1, :]
    return num / den


def _hgrn_prepare(qs, kk, lf2, loc_ref):
    c, d = qs.shape
    dg = HGRN_SUBBLOCK
    nblk = c // dg
    rep = SUBLANES // nblk
    kidx = lax.broadcasted_iota(jnp.int32, (SUBLANES, d), 0)
    slabs = []
    for j in range(c // SUBLANES):
        v = lf2[j * SUBLANES:(j + 1) * SUBLANES]
        sh = 1
        while sh < SUBLANES:
            v = v + jnp.where(kidx >= sh, pltpu.roll(v, sh, 0), 0.0)
            sh *= 2
        if j % (dg // SUBLANES):
            v = v + _row_bcast(slabs[-1], SUBLANES - 1, SUBLANES)
        slabs.append(v)
    loc = jnp.concatenate(slabs, axis=0)
    n_lane_tiles = d // LANES
    for t in range(n_lane_tiles):
        loc_ref[t] = loc[:, t * LANES:(t + 1) * LANES]
    step = dg // rep
    part = jnp.concatenate([loc_ref[t, pl.ds(step - 1, SUBLANES, stride=step), :]
                            for t in range(n_lane_tiles)], axis=1)
    tot = part
    sh = 1
    while sh < rep:
        tot = jnp.where((kidx & (rep - 1)) >= rep - sh, tot, pltpu.roll(tot, SUBLANES - sh, 0))
        sh *= 2
    r_end = tot
    sh = rep
    while sh < SUBLANES:
        r_end = r_end + jnp.where(kidx >= sh, pltpu.roll(r_end, sh, 0), 0.0)
        sh *= 2
    r_start = r_end - tot
    b_last = r_end[SUBLANES - 1:SUBLANES, :]

    tot_rows = jnp.concatenate([_row_bcast(tot, j * rep, dg) for j in range(nblk)], axis=0)
    q_hat = qs * jnp.exp2(loc)
    k_hat = kk * jnp.exp2(tot_rows - loc)
    q_dg = q_hat.astype(BF16)
    k_dg = (kk * jnp.exp2(jnp.minimum(-loc, EXP2_CLAMP))).astype(BF16)

    def scaled_f32(fac, use_q, unit):
        parts = []
        for j in range(nblk):
            rows_j = (q_hat if use_q[j] else k_hat)[j * dg:(j + 1) * dg]
            parts.append(rows_j if unit[j] else rows_j * _row_bcast(fac, j * rep, dg))
        return jnp.concatenate(parts, axis=0)

    def scaled(fac, use_q, unit):
        return scaled_f32(fac, use_q, unit).astype(BF16)

    q_in = scaled(jnp.exp2(r_start), [True] * nblk, [j == 0 for j in range(nblk)])
    k_end_t = scaled_f32(jnp.exp2(b_last - r_end), [False] * nblk,
                         [j == nblk - 1 for j in range(nblk)]).T.astype(BF16)
    decay_t = jnp.broadcast_to(jnp.exp2(b_last), (HGRN_DV, d)).T

    ti = lax.broadcasted_iota(jnp.int32, (c, c), 0)
    si = lax.broadcasted_iota(jnp.int32, (c, c), 1)
    levels = []
    hb = nblk // 2
    while hb >= 1:
        m = None
        for g in reversed(range(nblk // (2 * hb))):
            row = _row_bcast(r_start, (g * 2 * hb + hb) * rep, SUBLANES)
            m = row if m is None else jnp.where(kidx < (g + 1) * 2 * hb * rep, row, m)
        fac = jnp.exp2(jnp.where((kidx & (hb * rep)) != 0, r_start - m, m - r_end))
        unit = [(j % hb == 0) if (j & hb) else (j % hb == hb - 1) for j in range(nblk)]
        x = scaled(fac, [(j & hb) != 0 for j in range(nblk)], unit)
        half = hb * dg
        mask = ((ti ^ si) < 2 * half) & ((ti & half) != 0) & ((si & half) == 0)
        levels.append((x, mask))
        hb //= 2
    worst = jnp.max(jnp.max(-tot, axis=1, keepdims=True), axis=0, keepdims=True)
    mask_dg = ((ti ^ si) < dg) & (si <= ti) & (worst <= EXP2_CLAMP)
    return dict(q_in=q_in, k_end_t=k_end_t, decay_t=decay_t, levels=levels, q_dg=q_dg, k_dg=k_dg,
                mask_dg=mask_dg, flagged=worst[0, 0] > EXP2_CLAMP, loc=loc, qs=qs, kk=kk)


def _hgrn_intra(p):
    out = []
    for h in range(HGRN_HEADS):
        hs = slice(h * HGRN_DK, (h + 1) * HGRN_DK)
        a = jnp.where(p["mask_dg"],
                      lax.dot_general(p["q_dg"][:, hs], p["k_dg"][:, hs], _NT, preferred_element_type=F32), 0.0)
        for x, mask in p["levels"]:
            xh = x[:, hs]
            a = jnp.where(mask, lax.dot_general(xh, xh, _NT, preferred_element_type=F32), a)
        out.append(a.astype(BF16))
    return out


def _hgrn_apply(p, a_heads, iv, st_ref, n, o_ref):
    c = iv.shape[0]
    for h in range(HGRN_HEADS):
        hs = slice(h * HGRN_DK, (h + 1) * HGRN_DK)
        vs = slice(h * HGRN_DV, (h + 1) * HGRN_DV)
        st = st_ref[n, h]
        stacked = jnp.dot(jnp.concatenate([a_heads[h], p["k_end_t"][hs, :]], axis=0), iv[:, vs],
                          preferred_element_type=F32)
        o_ref[:, vs] = jnp.dot(p["q_in"][:, hs], st.astype(BF16), preferred_element_type=F32) + stacked[:c]
        st_ref[n, h] = st * p["decay_t"][hs, :] + stacked[c:]


def _hgrn_exact_diag(p, iv):
    qs, kk, loc = p["qs"], p["kk"], p["loc"]
    c, d = qs.shape
    dg = HGRN_SUBBLOCK
    ivf = iv.astype(F32)
    inblk = lax.broadcasted_iota(jnp.int32, (c, d), 0) & (dg - 1)

    def body(dd, acc):
        valid = inblk >= dd
        expo = jnp.where(valid, loc - pltpu.roll(loc, dd, 0), 0.0)
        w = jnp.where(valid, qs * pltpu.roll(kk, dd, 0) * jnp.exp2(expo), 0.0)
        ir = pltpu.roll(ivf, dd, 0)
        parts = []
        for h in range(HGRN_HEADS):
            hs = slice(h * HGRN_DK, (h + 1) * HGRN_DK)
            vs = slice(h * HGRN_DV, (h + 1) * HGRN_DV)
            parts.append(jnp.sum(w[:, hs], axis=-1, keepdims=True) * ir[:, vs])
        return acc + jnp.concatenate(parts, axis=1)

    return lax.fori_loop(0, dg, body, jnp.zeros((c, d), F32))


def _layer_body(cfg, *refs):
    refs = list(refs)
    x_ref, mod_ref = refs[:2]
    pos = 2
    s0_ref = None
    if cfg.has_s0:
        s0_ref = refs[pos]
        pos += 1
    (normg_ref, win_ref, lnvg_ref, lnvb_ref, ws_ref, bs_ref, lbraw_ref, gng_ref,
     wpa_ref, wpb_ref, wo_ref, fing_ref) = refs[pos:pos + 12]
    pos += 12
    pos += 2 if cfg.emit_v else 1
    y_ref, sout_ref = refs[pos:pos + 2]
    pos += 2
    v_ref = None
    if cfg.emit_v:
        v_ref = refs[pos]
        pos += 1
    st_ref, yb_ref, loc_ref, o_ref = refs[pos:pos + 4]

    nb, tt = cfg.nb, cfg.tt
    rows = nb * tt
    d = x_ref.shape[-1]
    t_idx = pl.program_id(1)
    n_t = pl.num_programs(1)

    @pl.when(t_idx == 0)
    def _():
        for n in range(nb):
            for hd in range(HGRN_HEADS):
                if cfg.has_s0:
                    st_ref[n, hd] = s0_ref[n, hd]
                else:
                    st_ref[n, hd] = jnp.zeros((HGRN_DK, HGRN_DV), F32)

    hs = []
    for n in range(nb):
        x = x_ref[n]
        shift = mod_ref[n, 0:1, :]
        scale = mod_ref[n, 1:2, :]
        hs.append(((x * _rms(x)) * (normg_ref[...] * (1.0 + scale)) + shift).astype(BF16))
    h = hs[0] if nb == 1 else jnp.concatenate(hs, axis=0)

    def proj(slab):
        return jnp.dot(h, _as_bf16(win_ref[:, slab * d:(slab + 1) * d]), preferred_element_type=F32)

    z = proj(SLAB_F)
    pq = proj(SLAB_Q)
    iv = proj(SLAB_I).astype(BF16)
    pv = proj(SLAB_V)
    pu = proj(SLAB_U)
    pza = proj(SLAB_ZA)

    e = _exp_neg(jnp.abs(z))
    inv = 1.0 / (1.0 + e)
    log2_sig = jnp.minimum(z, 0.0) * LOG2E - jnp.log2(1.0 + e)
    sig_neg = jnp.where(z >= 0.0, e, 1.0) * inv
    if cfg.layer == 0:
        lf2 = log2_sig
        kk = sig_neg
    else:
        lb = _forget_lower_bound(lbraw_ref, cfg.layer)
        f = lb + (1.0 - lb) * (jnp.where(z >= 0.0, 1.0, e) * inv)
        lf2 = jnp.where(f > 0.0, jnp.log2(f), log2_sig)
        kk = (1.0 - lb) * sig_neg
    qs = _silu(pq)

    v = _gelu_tanh(pv)
    mu = jnp.mean(v, axis=-1, keepdims=True)
    vc = v - mu
    var = jnp.mean(vc * vc, axis=-1, keepdims=True)
    vn = vc * lax.rsqrt(var + EPS) * lnvg_ref[...] + lnvb_ref[...]
    if cfg.emit_v:
        for n in range(nb):
            v_ref[n] = vn[n * tt:(n + 1) * tt]
    vb = vn.astype(BF16)
    gc = cfg.gc
    gdim = d // GMLP_GROUPS
    tril = (lax.broadcasted_iota(jnp.int32, (gc, gc), 0) >= lax.broadcasted_iota(jnp.int32, (gc, gc), 1))
    w_tril = [jnp.where(tril, ws_ref[g], 0.0).astype(BF16) for g in range(GMLP_GROUPS)]
    s_chunks = []
    for j in range(rows // gc):
        parts = [jnp.dot(w_tril[g], vb[j * gc:(j + 1) * gc, g * gdim:(g + 1) * gdim],
                         preferred_element_type=F32) + bs_ref[:, g:g + 1]
                 for g in range(GMLP_GROUPS)]
        s_chunks.append(jnp.concatenate(parts, axis=1))
    s = s_chunks[0] if len(s_chunks) == 1 else jnp.concatenate(s_chunks, axis=0)

    pzb = proj(SLAB_ZB)
    pga = proj(SLAB_GA)

    hc = cfg.hc
    cps = tt // hc
    n_chunks = rows // hc
    prep = [_hgrn_prepare(qs[j * hc:(j + 1) * hc], kk[j * hc:(j + 1) * hc], lf2[j * hc:(j + 1) * hc],
                          loc_ref.at[j])
            for j in range(n_chunks)]
    a_heads = [_hgrn_intra(p) for p in prep]

    ya = (_gelu_tanh(pu) * s * _silu(pza)).astype(BF16)
    m_a = _sigmoid(pga) * jnp.dot(ya, _as_bf16(wpa_ref[...]), preferred_element_type=F32)

    gate_b = _silu(pzb)
    gng = gng_ref[...]
    for j in range(n_chunks):
        _hgrn_apply(prep[j], a_heads[j], iv[j * hc:(j + 1) * hc], st_ref, j // cps, o_ref.at[j])
    sig_gb = _sigmoid(proj(SLAB_GB))

    def finish():
        for j in range(n_chunks):
            for hd in range(HGRN_HEADS):
                vs = slice(hd * HGRN_DV, (hd + 1) * HGRN_DV)
                o = o_ref[j, :, vs]
                yb_ref[j * hc:(j + 1) * hc, vs] = (o * _rms(o) * gng[:, vs]
                                                   * gate_b[j * hc:(j + 1) * hc, vs]).astype(BF16)
        m_b = sig_gb * jnp.dot(yb_ref[...], _as_bf16(wpb_ref[...]), preferred_element_type=F32)
        upd = jnp.dot((m_a + m_b).astype(BF16), _as_bf16(wo_ref[...]), preferred_element_type=F32)
        for n in range(nb):
            gate = mod_ref[n, 2:3, :]
            xo = x_ref[n] + gate * upd[n * tt:(n + 1) * tt]
            if cfg.final:
                xo = (xo * _rms(xo)) * fing_ref[...]
            y_ref[n] = xo

    finish()

    any_flagged = prep[0]["flagged"]
    for p in prep[1:]:
        any_flagged = jnp.logical_or(any_flagged, p["flagged"])

    @pl.when(any_flagged)
    def _():
        for j in range(n_chunks):
            @pl.when(prep[j]["flagged"])
            def _():
                o_ref[j] = o_ref[j] + _hgrn_exact_diag(prep[j], iv[j * hc:(j + 1) * hc])
        finish()

    @pl.when(t_idx == n_t - 1)
    def _():
        for n in range(nb):
            for hd in range(HGRN_HEADS):
                sout_ref[n, hd] = st_ref[n, hd]


def _tiling(batch, seq):
    chunk = min(seq, GMLP_CHUNK)
    assert seq % chunk == 0 and chunk % HGRN_SUBBLOCK == 0 and SUBLANES % (chunk // HGRN_SUBBLOCK) == 0
    if seq >= TILE_ROWS:
        assert seq % TILE_ROWS == 0
        return 1, TILE_ROWS, chunk
    nb = max(1, TILE_ROWS // seq)
    while batch % nb:
        nb -= 1
    return nb, seq, chunk


def _trunk_layer(layer, x, mod, s0, w, final, state_stack, v_stack):
    emit_v = v_stack is not None
    batch, seq, d = x.shape
    nb, tt, chunk = _tiling(batch, seq)
    hc = min(chunk, HGRN_CHUNK)
    n_b, n_t = batch // nb, seq // tt
    cfg = _Cfg(layer=layer, nb=nb, tt=tt, gc=chunk, hc=hc,
               has_s0=s0 is not None, emit_v=emit_v, final=final)
    rows = nb * tt

    def full(a):
        nd = a.ndim
        return pl.BlockSpec(a.shape, lambda b, t, _nd=nd: (0,) * _nd)

    tok_spec = pl.BlockSpec((nb, tt, d), lambda b, t: (b, t, 0))
    st_spec = pl.BlockSpec((None, nb, HGRN_HEADS, HGRN_DK, HGRN_DV), lambda b, t: (layer, b, 0, 0, 0))
    v_spec = pl.BlockSpec((None, nb, tt, d), lambda b, t: (layer, b, t, 0))
    ws = w["w_s"][:, :chunk, :chunk]
    bs_t = w["b_s"][:, :chunk].T
    params = [w["norm_g"], w["w_in"], w["ln_v_g"], w["ln_v_b"], ws, bs_t, w["lb_raw"], w["gnorm_g"],
              w["w_pa"], w["w_pb"], w["w_o"], w["final_g"]]

    args = [x, mod]
    in_specs = [tok_spec, pl.BlockSpec((nb, 3, d), lambda b, t: (b, 0, 0))]
    if s0 is not None:
        args.append(s0)
        in_specs.append(st_spec)
    args += params
    in_specs += [full(a) for a in params]
    stacks = [state_stack] + ([v_stack] if emit_v else [])
    aliases = {len(args) + i: 1 + i for i in range(len(stacks))}
    args += stacks
    in_specs += [pl.BlockSpec(memory_space=pl.ANY)] * len(stacks)
    out_shape = [jax.ShapeDtypeStruct(x.shape, F32)] + [jax.ShapeDtypeStruct(a.shape, F32) for a in stacks]
    out_specs = [tok_spec, st_spec] + ([v_spec] if emit_v else [])
    return pl.pallas_call(
        functools.partial(_layer_body, cfg),
        grid=(n_b, n_t),
        in_specs=in_specs,
        out_specs=out_specs,
        out_shape=out_shape,
        input_output_aliases=aliases,
        scratch_shapes=[pltpu.VMEM((nb, HGRN_HEADS, HGRN_DK, HGRN_DV), F32),
                        pltpu.VMEM((rows, d), BF16),
                        pltpu.VMEM((rows // hc, d // LANES, hc, LANES), F32),
                        pltpu.VMEM((rows // hc, hc, d), F32)],
        compiler_params=pltpu.CompilerParams(
            dimension_semantics=("arbitrary", "arbitrary"),
            vmem_limit_bytes=V7X_VMEM_LIMIT_BYTES),
        name=f"trunk_layer{layer}_{'sample' if emit_v else 'prompt'}",
    )(*args)


def _ada_body(c_ref, w_ref, b_ref, o_ref):
    c = c_ref[...]
    o_ref[0] = jnp.dot(_silu(c), w_ref[0], preferred_element_type=F32,
                       precision=lax.Precision.HIGHEST) + b_ref[0]


def _ada_modulation(c_all, w_ada, b_ada):
    depth, d, d3 = w_ada.shape
    n = c_all.shape[0]
    return pl.pallas_call(
        _ada_body,
        grid=(depth, d3 // d),
        in_specs=[pl.BlockSpec((n, d), lambda l, j: (0, 0)),
                  pl.BlockSpec((1, d, d), lambda l, j: (l, 0, j)),
                  pl.BlockSpec((1, 1, d), lambda l, j: (l, 0, j))],
        out_specs=pl.BlockSpec((1, n, d), lambda l, j: (l, 0, j)),
        out_shape=jax.ShapeDtypeStruct((depth, n, d3), F32),
        compiler_params=pltpu.CompilerParams(dimension_semantics=("arbitrary", "arbitrary")),
        name="ada_modulation",
    )(c_all, w_ada, b_ada.reshape(depth, 1, d3))


def kernel(x_prompt, x_sample, state_hgrn, c_prompt, c_sample, w_ada, b_ada, norm_g, w_in, ln_v_g, ln_v_b,
           w_s, b_s, lb_raw, gnorm_g, w_pa, w_pb, w_o, final_g):
    depth, d = norm_g.shape
    n_prompt = x_prompt.shape[0]
    mod = _ada_modulation(jnp.concatenate([c_prompt, c_sample], axis=0), w_ada, b_ada)
    mod = mod.reshape(depth, -1, 3, d)

    xp, xs = x_prompt, x_sample
    n_sample = x_sample.shape[0]
    sp = jnp.zeros((depth, n_prompt, HGRN_HEADS, HGRN_DK, HGRN_DV), F32)
    ss = jnp.zeros((depth, n_sample, HGRN_HEADS, HGRN_DK, HGRN_DV), F32)
    vs = jnp.zeros((depth,) + x_sample.shape, F32)
    for l in range(depth):
        w = dict(norm_g=norm_g[l][None], w_in=_pack_rows_bf16(w_in, l), ln_v_g=ln_v_g[l][None],
                 ln_v_b=ln_v_b[l][None], w_s=w_s[l], b_s=b_s[l], lb_raw=lb_raw, gnorm_g=gnorm_g[l][None],
                 w_pa=_pack_rows_bf16(w_pa, l), w_pb=_pack_rows_bf16(w_pb, l), w_o=_pack_rows_bf16(w_o, l),
                 final_g=final_g[None])
        final = l == depth - 1
        xp, sp = _trunk_layer(l, xp, mod[l, :n_prompt], None, w, final, sp, None)
        xs, ss, vs = _trunk_layer(l, xs, mod[l, n_prompt:], state_hgrn, w, final, ss, vs)
    return (xp, xs, sp, ss, vs)
```

```python
import functools
import math
from typing import NamedTuple

import jax
import jax.numpy as jnp
from jax import lax
from jax.experimental import pallas as pl
from jax.experimental.pallas import tpu as pltpu

F32 = jnp.float32
BF16 = jnp.bfloat16

EPS = 1e-6
LANES = 128
GMLP_CHUNK = 128
GMLP_GROUPS = 4
HGRN_HEADS = 8
HGRN_DK = 128
HGRN_DV = 128
N_IN_SLABS = 9
SLAB_U, SLAB_V, SLAB_ZA, SLAB_Q, SLAB_F, SLAB_I, SLAB_ZB, SLAB_GA, SLAB_GB = range(N_IN_SLABS)

HGRN_CHUNK = 128
HGRN_SUBBLOCK = 32
SUBLANES = 8
EXP2_CLAMP = 115.0
LOG2E = math.log2(math.e)
TILE_ROWS = 256
PACK_BLOCK_COLS = 1024
V7X_VMEM_LIMIT_BYTES = 60000 * 1024

_NT = (((1,), (1,)), ((), ()))


class _Cfg(NamedTuple):
    layer: int
    nb: int
    tt: int
    gc: int
    hc: int
    has_s0: bool
    emit_v: bool
    final: bool


def _gelu_tanh(x):
    c = math.sqrt(2.0 / math.pi)
    half_x = 0.5 * x
    return half_x + half_x * jnp.tanh(x * (c + (c * 0.044715) * (x * x)))


def _exp_neg(x):
    return jnp.exp2(x * (-LOG2E))


def _sigmoid(x):
    return 1.0 / (1.0 + _exp_neg(x))


def _silu(x):
    return x * _sigmoid(x)


def _rms(x):
    return lax.rsqrt(jnp.mean(x * x, axis=-1, keepdims=True) + EPS)


def _pack_body(*refs):
    n = len(refs) // 2
    for w_ref, o_ref in zip(refs[:n], refs[n:]):
        o_ref[...] = pltpu.bitcast(w_ref[...].astype(BF16), jnp.uint32)


def _pack_rows_bf16(w_stacks, layer):
    _, k, n = w_stacks[0].shape
    assert all(w.shape[1:] == (k, n) for w in w_stacks)
    bn = min(n, PACK_BLOCK_COLS)
    assert n % bn == 0
    return pl.pallas_call(
        _pack_body,
        grid=(n // bn,),
        in_specs=[pl.BlockSpec((None, k, bn), lambda j: (layer, 0, j))] * len(w_stacks),
        out_specs=[pl.BlockSpec((k // 2, bn), lambda j: (0, j))] * len(w_stacks),
        out_shape=[jax.ShapeDtypeStruct((k // 2, n), jnp.uint32)] * len(w_stacks),
        compiler_params=pltpu.CompilerParams(dimension_semantics=("arbitrary",)),
        name="pack_bf16_rows",
    )(*w_stacks)


def _as_bf16(packed):
    return pltpu.bitcast(packed, BF16)


def _row_bcast(a, idx, n_rep):
    return jnp.broadcast_to(a[idx:idx + 1, :], (n_rep, a.shape[1]))


def _forget_lower_bound(lbraw_ref, layer):
    raw = lbraw_ref[...]
    mx = jnp.max(raw, axis=0, keepdims=True)
    ex = jnp.exp(raw - mx)
    den = jnp.sum(ex, axis=0, keepdims=True)
    num = jnp.zeros_like(den)
    for j in range(1, layer + 1):
        num = num + ex[j:j + 1, :]
    return num / den


def _hgrn_prepare(qs, kk, lf2, loc_ref):
    c, d = qs.shape
    dg = HGRN_SUBBLOCK
    nblk = c // dg
    rep = SUBLANES // nblk
    kidx = lax.broadcasted_iota(jnp.int32, (SUBLANES, d), 0)
    slabs = []
    for j in range(c // SUBLANES):
        v = lf2[j * SUBLANES:(j + 1) * SUBLANES]
        sh = 1
        while sh < SUBLANES:
            v = v + jnp.where(kidx >= sh, pltpu.roll(v, sh, 0), 0.0)
            sh *= 2
        if j % (dg // SUBLANES):
            v = v + _row_bcast(slabs[-1], SUBLANES - 1, SUBLANES)
        slabs.append(v)
    loc = jnp.concatenate(slabs, axis=0)
    n_lane_tiles = d // LANES
    for t in range(n_lane_tiles):
        loc_ref[t] = loc[:, t * LANES:(t + 1) * LANES]
    step = dg // rep
    part = jnp.concatenate([loc_ref[t, pl.ds(step - 1, SUBLANES, stride=step), :]
                            for t in range(n_lane_tiles)], axis=1)
    tot = part
    sh = 1
    while sh < rep:
        tot = jnp.where((kidx & (rep - 1)) >= rep - sh, tot, pltpu.roll(tot, SUBLANES - sh, 0))
        sh *= 2
    r_end = tot
    sh = rep
    while sh < SUBLANES:
        r_end = r_end + jnp.where(kidx >= sh, pltpu.roll(r_end, sh, 0), 0.0)
        sh *= 2
    r_start = r_end - tot
    b_last = r_end[SUBLANES - 1:SUBLANES, :]

    tot_rows = jnp.concatenate([_row_bcast(tot, j * rep, dg) for j in range(nblk)], axis=0)
    q_hat = qs * jnp.exp2(loc)
    k_hat = kk * jnp.exp2(tot_rows - loc)
    q_dg = q_hat.astype(BF16)
    k_dg = (kk * jnp.exp2(jnp.minimum(-loc, EXP2_CLAMP))).astype(BF16)

    def scaled_f32(fac, use_q, unit):
        parts = []
        for j in range(nblk):
            rows_j = (q_hat if use_q[j] else k_hat)[j * dg:(j + 1) * dg]
            parts.append(rows_j if unit[j] else rows_j * _row_bcast(fac, j * rep, dg))
        return jnp.concatenate(parts, axis=0)

    def scaled(fac, use_q, unit):
        return scaled_f32(fac, use_q, unit).astype(BF16)

    q_in = scaled(jnp.exp2(r_start), [True] * nblk, [j == 0 for j in range(nblk)])
    k_end_t = scaled_f32(jnp.exp2(b_last - r_end), [False] * nblk,
                         [j == nblk - 1 for j in range(nblk)]).T.astype(BF16)
    decay_t = jnp.broadcast_to(jnp.exp2(b_last), (HGRN_DV, d)).T

    ti = lax.broadcasted_iota(jnp.int32, (c, c), 0)
    si = lax.broadcasted_iota(jnp.int32, (c, c), 1)
    levels = []
    hb = nblk // 2
    while hb >= 1:
        m = None
        for g in reversed(range(nblk // (2 * hb))):
            row = _row_bcast(r_start, (g * 2 * hb + hb) * rep, SUBLANES)
            m = row if m is None else jnp.where(kidx < (g + 1) * 2 * hb * rep, row, m)
        fac = jnp.exp2(jnp.where((kidx & (hb * rep)) != 0, r_start - m, m - r_end))
        unit = [(j % hb == 0) if (j & hb) else (j % hb == hb - 1) for j in range(nblk)]
        x = scaled(fac, [(j & hb) != 0 for j in range(nblk)], unit)
        half = hb * dg
        mask = ((ti ^ si) < 2 * half) & ((ti & half) != 0) & ((si & half) == 0)
        levels.append((x, mask))
        hb //= 2
    worst = jnp.max(jnp.max(-tot, axis=1, keepdims=True), axis=0, keepdims=True)
    mask_dg = ((ti ^ si) < dg) & (si <= ti) & (worst <= EXP2_CLAMP)
    return dict(q_in=q_in, k_end_t=k_end_t, decay_t=decay_t, levels=levels, q_dg=q_dg, k_dg=k_dg,
                mask_dg=mask_dg, flagged=worst[0, 0] > EXP2_CLAMP, loc=loc, qs=qs, kk=kk)


def _hgrn_intra(p):
    out = []
    for h in range(HGRN_HEADS):
        hs = slice(h * HGRN_DK, (h + 1) * HGRN_DK)
        a = jnp.where(p["mask_dg"],
                      lax.dot_general(p["q_dg"][:, hs], p["k_dg"][:, hs], _NT, preferred_element_type=F32), 0.0)
        for x, mask in p["levels"]:
            xh = x[:, hs]
            a = jnp.where(mask, lax.dot_general(xh, xh, _NT, preferred_element_type=F32), a)
        out.append(a.astype(BF16))
    return out


def _hgrn_apply(p, a_heads, iv, st_ref, n, o_ref):
    c = iv.shape[0]
    for h in range(HGRN_HEADS):
        hs = slice(h * HGRN_DK, (h + 1) * HGRN_DK)
        vs = slice(h * HGRN_DV, (h + 1) * HGRN_DV)
        st = st_ref[n, h]
        stacked = jnp.dot(jnp.concatenate([a_heads[h], p["k_end_t"][hs, :]], axis=0), iv[:, vs],
                          preferred_element_type=F32)
        o_ref[:, vs] = jnp.dot(p["q_in"][:, hs], st.astype(BF16), preferred_element_type=F32) + stacked[:c]
        st_ref[n, h] = st * p["decay_t"][hs, :] + stacked[c:]


def _hgrn_exact_diag(p, iv):
    qs, kk, loc = p["qs"], p["kk"], p["loc"]
    c, d = qs.shape
    dg = HGRN_SUBBLOCK
    ivf = iv.astype(F32)
    inblk = lax.broadcasted_iota(jnp.int32, (c, d), 0) & (dg - 1)

    def body(dd, acc):
        valid = inblk >= dd
        expo = jnp.where(valid, loc - pltpu.roll(loc, dd, 0), 0.0)
        w = jnp.where(valid, qs * pltpu.roll(kk, dd, 0) * jnp.exp2(expo), 0.0)
        ir = pltpu.roll(ivf, dd, 0)
        parts = []
        for h in range(HGRN_HEADS):
            hs = slice(h * HGRN_DK, (h + 1) * HGRN_DK)
            vs = slice(h * HGRN_DV, (h + 1) * HGRN_DV)
            parts.append(jnp.sum(w[:, hs], axis=-1, keepdims=True) * ir[:, vs])
        return acc + jnp.concatenate(parts, axis=1)

    return lax.fori_loop(0, dg, body, jnp.zeros((c, d), F32))


def _layer_body(cfg, *refs):
    refs = list(refs)
    x_ref, mod_ref = refs[:2]
    pos = 2
    s0_ref = None
    if cfg.has_s0:
        s0_ref = refs[pos]
        pos += 1
    (normg_ref, win_ref, lnvg_ref, lnvb_ref, ws_ref, bs_ref, lbraw_ref, gng_ref,
     wpa_ref, wpb_ref, wo_ref, fing_ref) = refs[pos:pos + 12]
    pos += 12
    pos += 2 if cfg.emit_v else 1
    y_ref, sout_ref = refs[pos:pos + 2]
    pos += 2
    v_ref = None
    if cfg.emit_v:
        v_ref = refs[pos]
        pos += 1
    st_ref, yb_ref, loc_ref, o_ref = refs[pos:pos + 4]

    nb, tt = cfg.nb, cfg.tt
    rows = nb * tt
    d = x_ref.shape[-1]
    t_idx = pl.program_id(1)
    n_t = pl.num_programs(1)

    @pl.when(t_idx == 0)
    def _():
        for n in range(nb):
            for hd in range(HGRN_HEADS):
                if cfg.has_s0:
                    st_ref[n, hd] = s0_ref[n, hd]
                else:
                    st_ref[n, hd] = jnp.zeros((HGRN_DK, HGRN_DV), F32)

    hs = []
    for n in range(nb):
        x = x_ref[n]
        shift = mod_ref[n, 0:1, :]
        scale = mod_ref[n, 1:2, :]
        hs.append(((x * _rms(x)) * (normg_ref[...] * (1.0 + scale)) + shift).astype(BF16))
    h = hs[0] if nb == 1 else jnp.concatenate(hs, axis=0)

    def proj(slab):
        return jnp.dot(h, _as_bf16(win_ref[:, slab * d:(slab + 1) * d]), preferred_element_type=F32)

    z = proj(SLAB_F)
    pq = proj(SLAB_Q)
    iv = proj(SLAB_I).astype(BF16)
    pv = proj(SLAB_V)
    pu = proj(SLAB_U)
    pza = proj(SLAB_ZA)

    e = _exp_neg(jnp.abs(z))
    inv = 1.0 / (1.0 + e)
    log2_sig = jnp.minimum(z, 0.0) * LOG2E - jnp.log2(1.0 + e)
    sig_neg = jnp.where(z >= 0.0, e, 1.0) * inv
    if cfg.layer == 0:
        lf2 = log2_sig
        kk = sig_neg
    else:
        lb = _forget_lower_bound(lbraw_ref, cfg.layer)
        f = lb + (1.0 - lb) * (jnp.where(z >= 0.0, 1.0, e) * inv)
        lf2 = jnp.where(f > 0.0, jnp.log2(f), log2_sig)
        kk = (1.0 - lb) * sig_neg
    qs = _silu(pq)

    v = _gelu_tanh(pv)
    mu = jnp.mean(v, axis=-1, keepdims=True)
    vc = v - mu
    var = jnp.mean(vc * vc, axis=-1, keepdims=True)
    vn = vc * lax.rsqrt(var + EPS) * lnvg_ref[...] + lnvb_ref[...]
    if cfg.emit_v:
        for n in range(nb):
            v_ref[n] = vn[n * tt:(n + 1) * tt]
    vb = vn.astype(BF16)
    gc = cfg.gc
    gdim = d // GMLP_GROUPS
    tril = (lax.broadcasted_iota(jnp.int32, (gc, gc), 0) >= lax.broadcasted_iota(jnp.int32, (gc, gc), 1))
    w_tril = [jnp.where(tril, ws_ref[g], 0.0).astype(BF16) for g in range(GMLP_GROUPS)]
    s_chunks = []
    for j in range(rows // gc):
        parts = [jnp.dot(w_tril[g], vb[j * gc:(j + 1) * gc, g * gdim:(g + 1) * gdim],
                         preferred_element_type=F32) + bs_ref[:, g:g + 1]
                 for g in range(GMLP_GROUPS)]
        s_chunks.append(jnp.concatenate(parts, axis=1))
    s = s_chunks[0] if len(s_chunks) == 1 else jnp.concatenate(s_chunks, axis=0)

    pzb = proj(SLAB_ZB)
    pga = proj(SLAB_GA)

    hc = cfg.hc
    cps = tt // hc
    n_chunks = rows // hc
    prep = [_hgrn_prepare(qs[j * hc:(j + 1) * hc], kk[j * hc:(j + 1) * hc], lf2[j * hc:(j + 1) * hc],
                          loc_ref.at[j])
            for j in range(n_chunks)]
    a_heads = [_hgrn_intra(p) for p in prep]

    ya = (_gelu_tanh(pu) * s * _silu(pza)).astype(BF16)
    m_a = _sigmoid(pga) * jnp.dot(ya, _as_bf16(wpa_ref[...]), preferred_element_type=F32)

    gate_b = _silu(pzb)
    gng = gng_ref[...]
    for j in range(n_chunks):
        _hgrn_apply(prep[j], a_heads[j], iv[j * hc:(j + 1) * hc], st_ref, j // cps, o_ref.at[j])
    sig_gb = _sigmoid(proj(SLAB_GB))

    def finish():
        for j in range(n_chunks):
            for hd in range(HGRN_HEADS):
                vs = slice(hd * HGRN_DV, (hd + 1) * HGRN_DV)
                o = o_ref[j, :, vs]
                yb_ref[j * hc:(j + 1) * hc, vs] = (o * _rms(o) * gng[:, vs]
                                                   * gate_b[j * hc:(j + 1) * hc, vs]).astype(BF16)
        m_b = sig_gb * jnp.dot(yb_ref[...], _as_bf16(wpb_ref[...]), preferred_element_type=F32)
        upd = jnp.dot((m_a + m_b).astype(BF16), _as_bf16(wo_ref[...]), preferred_element_type=F32)
        for n in range(nb):
            gate = mod_ref[n, 2:3, :]
            xo = x_ref[n] + gate * upd[n * tt:(n + 1) * tt]
            if cfg.final:
                xo = (xo * _rms(xo)) * fing_ref[...]
            y_ref[n] = xo

    finish()

    any_flagged = prep[0]["flagged"]
    for p in prep[1:]:
        any_flagged = jnp.logical_or(any_flagged, p["flagged"])

    @pl.when(any_flagged)
    def _():
        for j in range(n_chunks):
            @pl.when(prep[j]["flagged"])
            def _():
                o_ref[j] = o_ref[j] + _hgrn_exact_diag(prep[j], iv[j * hc:(j + 1) * hc])
        finish()

    @pl.when(t_idx == n_t - 1)
    def _():
        for n in range(nb):
            for hd in range(HGRN_HEADS):
                sout_ref[n, hd] = st_ref[n, hd]


def _tiling(batch, seq):
    chunk = min(seq, GMLP_CHUNK)
    assert seq % chunk == 0 and chunk % HGRN_SUBBLOCK == 0 and SUBLANES % (chunk // HGRN_SUBBLOCK) == 0
    if seq >= TILE_ROWS:
        assert seq % TILE_ROWS == 0
        return 1, TILE_ROWS, chunk
    nb = max(1, TILE_ROWS // seq)
    while batch % nb:
        nb -= 1
    return nb, seq, chunk


def _trunk_layer(layer, x, mod, s0, w, final, state_stack, v_stack):
    emit_v = v_stack is not None
    batch, seq, d = x.shape
    nb, tt, chunk = _tiling(batch, seq)
    hc = min(chunk, HGRN_CHUNK)
    n_b, n_t = batch // nb, seq // tt
    cfg = _Cfg(layer=layer, nb=nb, tt=tt, gc=chunk, hc=hc,
               has_s0=s0 is not None, emit_v=emit_v, final=final)
    rows = nb * tt

    def full(a):
        nd = a.ndim
        return pl.BlockSpec(a.shape, lambda b, t, _nd=nd: (0,) * _nd)

    tok_spec = pl.BlockSpec((nb, tt, d), lambda b, t: (b, t, 0))
    st_spec = pl.BlockSpec((None, nb, HGRN_HEADS, HGRN_DK, HGRN_DV), lambda b, t: (layer, b, 0, 0, 0))
    v_spec = pl.BlockSpec((None, nb, tt, d), lambda b, t: (layer, b, t, 0))
    ws = w["w_s"][:, :chunk, :chunk]
    bs_t = w["b_s"][:, :chunk].T
    params = [w["norm_g"], w["w_in"], w["ln_v_g"], w["ln_v_b"], ws, bs_t, w["lb_raw"], w["gnorm_g"],
              w["w_pa"], w["w_pb"], w["w_o"], w["final_g"]]

    args = [x, mod]
    in_specs = [tok_spec, pl.BlockSpec((nb, 3, d), lambda b, t: (b, 0, 0))]
    if s0 is not None:
        args.append(s0)
        in_specs.append(st_spec)
    args += params
    in_specs += [full(a) for a in params]
    stacks = [state_stack] + ([v_stack] if emit_v else [])
    aliases = {len(args) + i: 1 + i for i in range(len(stacks))}
    args += stacks
    in_specs += [pl.BlockSpec(memory_space=pl.ANY)] * len(stacks)
    out_shape = [jax.ShapeDtypeStruct(x.shape, F32)] + [jax.ShapeDtypeStruct(a.shape, F32) for a in stacks]
    out_specs = [tok_spec, st_spec] + ([v_spec] if emit_v else [])
    return pl.pallas_call(
        functools.partial(_layer_body, cfg),
        grid=(n_b, n_t),
        in_specs=in_specs,
        out_specs=out_specs,
        out_shape=out_shape,
        input_output_aliases=aliases,
        scratch_shapes=[pltpu.VMEM((nb, HGRN_HEADS, HGRN_DK, HGRN_DV), F32),
                        pltpu.VMEM((rows, d), BF16),
                        pltpu.VMEM((rows // hc, d // LANES, hc, LANES), F32),
                        pltpu.VMEM((rows // hc, hc, d), F32)],
        compiler_params=pltpu.CompilerParams(
            dimension_semantics=("arbitrary", "arbitrary"),
            vmem_limit_bytes=V7X_VMEM_LIMIT_BYTES),
        name=f"trunk_layer{layer}_{'sample' if emit_v else 'prompt'}",
    )(*args)


def _split_bf16(a):
    hi = a.astype(BF16)
    return hi, (a - hi.astype(F32)).astype(BF16)


def _ada_body(c_ref, w_ref, b_ref, o_ref):
    n = c_ref.shape[0]
    s_hi, s_lo = _split_bf16(_silu(c_ref[...]))
    w_hi, w_lo = _split_bf16(w_ref[0])
    both = jnp.dot(jnp.concatenate([s_hi, s_lo], axis=0), w_hi, preferred_element_type=F32)
    o_ref[0] = both[:n] + both[n:] + jnp.dot(s_hi, w_lo, preferred_element_type=F32) + b_ref[0]


def _ada_modulation(c_all, w_ada, b_ada):
    depth, d, d3 = w_ada.shape
    n = c_all.shape[0]
    return pl.pallas_call(
        _ada_body,
        grid=(depth, d3 // d),
        in_specs=[pl.BlockSpec((n, d), lambda l, j: (0, 0)),
                  pl.BlockSpec((1, d, d), lambda l, j: (l, 0, j)),
                  pl.BlockSpec((1, 1, d), lambda l, j: (l, 0, j))],
        out_specs=pl.BlockSpec((1, n, d), lambda l, j: (l, 0, j)),
        out_shape=jax.ShapeDtypeStruct((depth, n, d3), F32),
        compiler_params=pltpu.CompilerParams(dimension_semantics=("arbitrary", "arbitrary")),
        name="ada_modulation",
    )(c_all, w_ada, b_ada.reshape(depth, 1, d3))


def kernel(x_prompt, x_sample, state_hgrn, c_prompt, c_sample, w_ada, b_ada, norm_g, w_in, ln_v_g, ln_v_b,
           w_s, b_s, lb_raw, gnorm_g, w_pa, w_pb, w_o, final_g):
    depth, d = norm_g.shape
    n_prompt = x_prompt.shape[0]
    mod = _ada_modulation(jnp.concatenate([c_prompt, c_sample], axis=0), w_ada, b_ada)
    mod = mod.reshape(depth, -1, 3, d)

    xp, xs = x_prompt, x_sample
    n_sample = x_sample.shape[0]
    sp = jnp.zeros((depth, n_prompt, HGRN_HEADS, HGRN_DK, HGRN_DV), F32)
    ss = jnp.zeros((depth, n_sample, HGRN_HEADS, HGRN_DK, HGRN_DV), F32)
    vs = jnp.zeros((depth,) + x_sample.shape, F32)
    for l in range(depth):
        (w_in_l,) = _pack_rows_bf16([w_in], l)
        w_pa_l, w_pb_l, w_o_l = _pack_rows_bf16([w_pa, w_pb, w_o], l)
        w = dict(norm_g=norm_g[l][None], w_in=w_in_l, ln_v_g=ln_v_g[l][None],
                 ln_v_b=ln_v_b[l][None], w_s=w_s[l], b_s=b_s[l], lb_raw=lb_raw, gnorm_g=gnorm_g[l][None],
                 w_pa=w_pa_l, w_pb=w_pb_l, w_o=w_o_l, final_g=final_g[None])
        final = l == depth - 1
        xp, sp = _trunk_layer(l, xp, mod[l, :n_prompt], None, w, final, sp, None)
        xs, ss, vs = _trunk_layer(l, xs, mod[l, n_prompt:], state_hgrn, w, final, ss, vs)
    return (xp, xs, sp, ss, vs)
```

```python
import functools
import math
from typing import NamedTuple

import jax
import jax.numpy as jnp
from jax import lax
from jax.experimental import pallas as pl
from jax.experimental.pallas import tpu as pltpu

F32 = jnp.float32
BF16 = jnp.bfloat16

EPS = 1e-6
LANES = 128
GMLP_CHUNK = 128
GMLP_GROUPS = 4
HGRN_HEADS = 8
HGRN_DK = 128
HGRN_DV = 128
N_IN_SLABS = 9
SLAB_U, SLAB_V, SLAB_ZA, SLAB_Q, SLAB_F, SLAB_I, SLAB_ZB, SLAB_GA, SLAB_GB = range(N_IN_SLABS)

HGRN_CHUNK = 128
HGRN_SUBBLOCK = 32
SUBLANES = 8
EXP2_CLAMP = 115.0
LOG2E = math.log2(math.e)
TILE_ROWS = 256
PACK_BLOCK_COLS = 1024
V7X_VMEM_LIMIT_BYTES = 60000 * 1024

_NT = (((1,), (1,)), ((), ()))


class _Cfg(NamedTuple):
    layer: int
    nb: int
    tt: int
    gc: int
    hc: int
    has_s0: bool
    emit_v: bool
    final: bool


def _gelu_tanh(x):
    c = math.sqrt(2.0 / math.pi)
    half_x = 0.5 * x
    return half_x + half_x * jnp.tanh(x * (c + (c * 0.044715) * (x * x)))


def _exp_neg(x):
    return jnp.exp2(x * (-LOG2E))


def _sigmoid(x):
    return 1.0 / (1.0 + _exp_neg(x))


def _silu(x):
    return x * _sigmoid(x)


def _rms(x):
    return lax.rsqrt(jnp.mean(x * x, axis=-1, keepdims=True) + EPS)


def _pack_body(*refs):
    n = len(refs) // 2
    for w_ref, o_ref in zip(refs[:n], refs[n:]):
        o_ref[...] = pltpu.bitcast(w_ref[...].astype(BF16), jnp.uint32)


def _pack_rows_bf16(w_stacks, layer):
    _, k, n = w_stacks[0].shape
    assert all(w.shape[1:] == (k, n) for w in w_stacks)
    bn = min(n, PACK_BLOCK_COLS)
    assert n % bn == 0
    return pl.pallas_call(
        _pack_body,
        grid=(n // bn,),
        in_specs=[pl.BlockSpec((None, k, bn), lambda j: (layer, 0, j))] * len(w_stacks),
        out_specs=[pl.BlockSpec((k // 2, bn), lambda j: (0, j))] * len(w_stacks),
        out_shape=[jax.ShapeDtypeStruct((k // 2, n), jnp.uint32)] * len(w_stacks),
        compiler_params=pltpu.CompilerParams(dimension_semantics=("arbitrary",)),
        name="pack_bf16_rows",
    )(*w_stacks)


def _as_bf16(packed):
    return pltpu.bitcast(packed, BF16)


def _row_bcast(a, idx, n_rep):
    return jnp.broadcast_to(a[idx:idx + 1, :], (n_rep, a.shape[1]))


def _forget_lower_bound(lbraw_ref, layer):
    raw = lbraw_ref[...]
    mx = jnp.max(raw, axis=0, keepdims=True)
    ex = jnp.exp(raw - mx)
    den = jnp.sum(ex, axis=0, keepdims=True)
    num = jnp.zeros_like(den)
    for j in range(1, layer + 1):
        num = num + ex[j:j + 1, :]
    return num / den


def _hgrn_prepare(qs, kk, lf2, loc_ref):
    c, d = qs.shape
    dg = HGRN_SUBBLOCK
    nblk = c // dg
    rep = SUBLANES // nblk
    kidx = lax.broadcasted_iota(jnp.int32, (SUBLANES, d), 0)
    slabs = []
    for j in range(c // SUBLANES):
        v = lf2[j * SUBLANES:(j + 1) * SUBLANES]
        sh = 1
        while sh < SUBLANES:
            v = v + jnp.where(kidx >= sh, pltpu.roll(v, sh, 0), 0.0)
            sh *= 2
        if j % (dg // SUBLANES):
            v = v + _row_bcast(slabs[-1], SUBLANES - 1, SUBLANES)
        slabs.append(v)
    loc = jnp.concatenate(slabs, axis=0)
    n_lane_tiles = d // LANES
    for t in range(n_lane_tiles):
        loc_ref[t] = loc[:, t * LANES:(t + 1) * LANES]
    step = dg // rep
    part = jnp.concatenate([loc_ref[t, pl.ds(step - 1, SUBLANES, stride=step), :]
                            for t in range(n_lane_tiles)], axis=1)
    tot = part
    sh = 1
    while sh < rep:
        tot = jnp.where((kidx & (rep - 1)) >= rep - sh, tot, pltpu.roll(tot, SUBLANES - sh, 0))
        sh *= 2
    r_end = tot
    sh = rep
    while sh < SUBLANES:
        r_end = r_end + jnp.where(kidx >= sh, pltpu.roll(r_end, sh, 0), 0.0)
        sh *= 2
    r_start = r_end - tot
    b_last = r_end[SUBLANES - 1:SUBLANES, :]

    tot_rows = jnp.concatenate([_row_bcast(tot, j * rep, dg) for j in range(nblk)], axis=0)
    q_hat = qs * jnp.exp2(loc)
    k_hat = kk * jnp.exp2(tot_rows - loc)
    q_dg = q_hat.astype(BF16)
    k_dg = (kk * jnp.exp2(jnp.minimum(-loc, EXP2_CLAMP))).astype(BF16)

    def scaled_f32(fac, use_q, unit):
        parts = []
        for j in range(nblk):
            rows_j = (q_hat if use_q[j] else k_hat)[j * dg:(j + 1) * dg]
            parts.append(rows_j if unit[j] else rows_j * _row_bcast(fac, j * rep, dg))
        return jnp.concatenate(parts, axis=0)

    def scaled(fac, use_q, unit):
        return scaled_f32(fac, use_q, unit).astype(BF16)

    q_in = scaled(jnp.exp2(r_start), [True] * nblk, [j == 0 for j in range(nblk)])
    k_end_t = scaled_f32(jnp.exp2(b_last - r_end), [False] * nblk,
                         [j == nblk - 1 for j in range(nblk)]).T.astype(BF16)
    decay_t = jnp.broadcast_to(jnp.exp2(b_last), (HGRN_DV, d)).T

    ti = lax.broadcasted_iota(jnp.int32, (c, c), 0)
    si = lax.broadcasted_iota(jnp.int32, (c, c), 1)
    levels = []
    hb = nblk // 2
    while hb >= 1:
        m = None
        for g in reversed(range(nblk // (2 * hb))):
            row = _row_bcast(r_start, (g * 2 * hb + hb) * rep, SUBLANES)
            m = row if m is None else jnp.where(kidx < (g + 1) * 2 * hb * rep, row, m)
        fac = jnp.exp2(jnp.where((kidx & (hb * rep)) != 0, r_start - m, m - r_end))
        unit = [(j % hb == 0) if (j & hb) else (j % hb == hb - 1) for j in range(nblk)]
        x = scaled(fac, [(j & hb) != 0 for j in range(nblk)], unit)
        half = hb * dg
        mask = ((ti ^ si) < 2 * half) & ((ti & half) != 0) & ((si & half) == 0)
        levels.append((x, mask))
        hb //= 2
    worst = jnp.max(jnp.max(-tot, axis=1, keepdims=True), axis=0, keepdims=True)
    mask_dg = ((ti ^ si) < dg) & (si <= ti) & (worst <= EXP2_CLAMP)
    return dict(q_in=q_in, k_end_t=k_end_t, decay_t=decay_t, levels=levels, q_dg=q_dg, k_dg=k_dg,
                mask_dg=mask_dg, flagged=worst[0, 0] > EXP2_CLAMP, loc=loc, qs=qs, kk=kk)


def _hgrn_intra(p):
    out = []
    for h in range(HGRN_HEADS):
        hs = slice(h * HGRN_DK, (h + 1) * HGRN_DK)
        a = jnp.where(p["mask_dg"],
                      lax.dot_general(p["q_dg"][:, hs], p["k_dg"][:, hs], _NT, preferred_element_type=F32), 0.0)
        for x, mask in p["levels"]:
            xh = x[:, hs]
            a = jnp.where(mask, lax.dot_general(xh, xh, _NT, preferred_element_type=F32), a)
        out.append(a.astype(BF16))
    return out


def _hgrn_apply(p, a_heads, iv, st_ref, n, o_ref):
    c = iv.shape[0]
    for h in range(HGRN_HEADS):
        hs = slice(h * HGRN_DK, (h + 1) * HGRN_DK)
        vs = slice(h * HGRN_DV, (h + 1) * HGRN_DV)
        st = st_ref[n, h]
        stacked = jnp.dot(jnp.concatenate([a_heads[h], p["k_end_t"][hs, :]], axis=0), iv[:, vs],
                          preferred_element_type=F32)
        o_ref[:, vs] = jnp.dot(p["q_in"][:, hs], st.astype(BF16), preferred_element_type=F32) + stacked[:c]
        st_ref[n, h] = st * p["decay_t"][hs, :] + stacked[c:]


def _hgrn_exact_diag(p, iv):
    qs, kk, loc = p["qs"], p["kk"], p["loc"]
    c, d = qs.shape
    dg = HGRN_SUBBLOCK
    ivf = iv.astype(F32)
    inblk = lax.broadcasted_iota(jnp.int32, (c, d), 0) & (dg - 1)

    def body(dd, acc):
        valid = inblk >= dd
        expo = jnp.where(valid, loc - pltpu.roll(loc, dd, 0), 0.0)
        w = jnp.where(valid, qs * pltpu.roll(kk, dd, 0) * jnp.exp2(expo), 0.0)
        ir = pltpu.roll(ivf, dd, 0)
        parts = []
        for h in range(HGRN_HEADS):
            hs = slice(h * HGRN_DK, (h + 1) * HGRN_DK)
            vs = slice(h * HGRN_DV, (h + 1) * HGRN_DV)
            parts.append(jnp.sum(w[:, hs], axis=-1, keepdims=True) * ir[:, vs])
        return acc + jnp.concatenate(parts, axis=1)

    return lax.fori_loop(0, dg, body, jnp.zeros((c, d), F32))


def _layer_body(cfg, *refs):
    refs = list(refs)
    x_ref, mod_ref = refs[:2]
    pos = 2
    s0_ref = None
    if cfg.has_s0:
        s0_ref = refs[pos]
        pos += 1
    (normg_ref, win_ref, lnvg_ref, lnvb_ref, ws_ref, bs_ref, lbraw_ref, gng_ref,
     wpa_ref, wpb_ref, wo_ref, fing_ref) = refs[pos:pos + 12]
    pos += 12
    pos += 2 if cfg.emit_v else 1
    y_ref, sout_ref = refs[pos:pos + 2]
    pos += 2
    v_ref = None
    if cfg.emit_v:
        v_ref = refs[pos]
        pos += 1
    st_ref, yb_ref, loc_ref, o_ref = refs[pos:pos + 4]

    nb, tt = cfg.nb, cfg.tt
    rows = nb * tt
    d = x_ref.shape[-1]
    t_idx = pl.program_id(1)
    n_t = pl.num_programs(1)

    @pl.when(t_idx == 0)
    def _():
        for n in range(nb):
            for hd in range(HGRN_HEADS):
                if cfg.has_s0:
                    st_ref[n, hd] = s0_ref[n, hd]
                else:
                    st_ref[n, hd] = jnp.zeros((HGRN_DK, HGRN_DV), F32)

    hs = []
    for n in range(nb):
        x = x_ref[n]
        shift = mod_ref[n, 0:1, :]
        scale = mod_ref[n, 1:2, :]
        hs.append(((x * _rms(x)) * (normg_ref[...] * (1.0 + scale)) + shift).astype(BF16))
    h = hs[0] if nb == 1 else jnp.concatenate(hs, axis=0)

    def proj(slab):
        return jnp.dot(h, _as_bf16(win_ref[:, slab * d:(slab + 1) * d]), preferred_element_type=F32)

    z = proj(SLAB_F)
    pq = proj(SLAB_Q)
    iv = proj(SLAB_I).astype(BF16)
    pv = proj(SLAB_V)
    pu = proj(SLAB_U)
    pza = proj(SLAB_ZA)

    t_neg = jnp.abs(z) * (-LOG2E)
    e = jnp.exp2(t_neg)
    inv = 1.0 / (1.0 + e)
    log2_sig = jnp.where(z >= 0.0, 0.0, t_neg) + jnp.log2(inv)
    sig_neg = jnp.where(z >= 0.0, e, 1.0) * inv
    if cfg.layer == 0:
        lf2 = log2_sig
        kk = sig_neg
    else:
        lb = _forget_lower_bound(lbraw_ref, cfg.layer)
        f = lb + (1.0 - lb) * (jnp.where(z >= 0.0, 1.0, e) * inv)
        lf2 = jnp.where(f > 0.0, jnp.log2(f), log2_sig)
        kk = (1.0 - lb) * sig_neg
    qs = _silu(pq)

    v = _gelu_tanh(pv)
    mu = jnp.mean(v, axis=-1, keepdims=True)
    vc = v - mu
    var = jnp.mean(vc * vc, axis=-1, keepdims=True)
    vn = vc * lax.rsqrt(var + EPS) * lnvg_ref[...] + lnvb_ref[...]
    if cfg.emit_v:
        for n in range(nb):
            v_ref[n] = vn[n * tt:(n + 1) * tt]
    vb = vn.astype(BF16)
    gc = cfg.gc
    gdim = d // GMLP_GROUPS
    tril = (lax.broadcasted_iota(jnp.int32, (gc, gc), 0) >= lax.broadcasted_iota(jnp.int32, (gc, gc), 1))
    w_tril = [jnp.where(tril, ws_ref[g], 0.0).astype(BF16) for g in range(GMLP_GROUPS)]
    s_chunks = []
    for j in range(rows // gc):
        parts = [jnp.dot(w_tril[g], vb[j * gc:(j + 1) * gc, g * gdim:(g + 1) * gdim],
                         preferred_element_type=F32) + bs_ref[:, g:g + 1]
                 for g in range(GMLP_GROUPS)]
        s_chunks.append(jnp.concatenate(parts, axis=1))
    s = s_chunks[0] if len(s_chunks) == 1 else jnp.concatenate(s_chunks, axis=0)

    pzb = proj(SLAB_ZB)
    pga = proj(SLAB_GA)

    hc = cfg.hc
    cps = tt // hc
    n_chunks = rows // hc
    prep = [_hgrn_prepare(qs[j * hc:(j + 1) * hc], kk[j * hc:(j + 1) * hc], lf2[j * hc:(j + 1) * hc],
                          loc_ref.at[j])
            for j in range(n_chunks)]
    a_heads = [_hgrn_intra(p) for p in prep]

    ya = (_gelu_tanh(pu) * s * _silu(pza)).astype(BF16)
    m_a = _sigmoid(pga) * jnp.dot(ya, _as_bf16(wpa_ref[...]), preferred_element_type=F32)

    gate_b = _silu(pzb)
    gng = gng_ref[...]
    for j in range(n_chunks):
        _hgrn_apply(prep[j], a_heads[j], iv[j * hc:(j + 1) * hc], st_ref, j // cps, o_ref.at[j])
    sig_gb = _sigmoid(proj(SLAB_GB))

    def finish():
        for j in range(n_chunks):
            for hd in range(HGRN_HEADS):
                vs = slice(hd * HGRN_DV, (hd + 1) * HGRN_DV)
                o = o_ref[j, :, vs]
                yb_ref[j * hc:(j + 1) * hc, vs] = (o * _rms(o) * gng[:, vs]
                                                   * gate_b[j * hc:(j + 1) * hc, vs]).astype(BF16)
        m_b = sig_gb * jnp.dot(yb_ref[...], _as_bf16(wpb_ref[...]), preferred_element_type=F32)
        upd = jnp.dot((m_a + m_b).astype(BF16), _as_bf16(wo_ref[...]), preferred_element_type=F32)
        for n in range(nb):
            gate = mod_ref[n, 2:3, :]
            xo = x_ref[n] + gate * upd[n * tt:(n + 1) * tt]
            if cfg.final:
                xo = (xo * _rms(xo)) * fing_ref[...]
            y_ref[n] = xo

    finish()

    any_flagged = prep[0]["flagged"]
    for p in prep[1:]:
        any_flagged = jnp.logical_or(any_flagged, p["flagged"])

    @pl.when(any_flagged)
    def _():
        for j in range(n_chunks):
            @pl.when(prep[j]["flagged"])
            def _():
                o_ref[j] = o_ref[j] + _hgrn_exact_diag(prep[j], iv[j * hc:(j + 1) * hc])
        finish()

    @pl.when(t_idx == n_t - 1)
    def _():
        for n in range(nb):
            for hd in range(HGRN_HEADS):
                sout_ref[n, hd] = st_ref[n, hd]


def _tiling(batch, seq):
    chunk = min(seq, GMLP_CHUNK)
    assert seq % chunk == 0 and chunk % HGRN_SUBBLOCK == 0 and SUBLANES % (chunk // HGRN_SUBBLOCK) == 0
    if seq >= TILE_ROWS:
        assert seq % TILE_ROWS == 0
        return 1, TILE_ROWS, chunk
    nb = max(1, TILE_ROWS // seq)
    while batch % nb:
        nb -= 1
    return nb, seq, chunk


def _trunk_layer(layer, x, mod, s0, w, final, state_stack, v_stack):
    emit_v = v_stack is not None
    batch, seq, d = x.shape
    nb, tt, chunk = _tiling(batch, seq)
    hc = min(chunk, HGRN_CHUNK)
    n_b, n_t = batch // nb, seq // tt
    cfg = _Cfg(layer=layer, nb=nb, tt=tt, gc=chunk, hc=hc,
               has_s0=s0 is not None, emit_v=emit_v, final=final)
    rows = nb * tt

    def full(a):
        nd = a.ndim
        return pl.BlockSpec(a.shape, lambda b, t, _nd=nd: (0,) * _nd)

    tok_spec = pl.BlockSpec((nb, tt, d), lambda b, t: (b, t, 0))
    st_spec = pl.BlockSpec((None, nb, HGRN_HEADS, HGRN_DK, HGRN_DV), lambda b, t: (layer, b, 0, 0, 0))
    v_spec = pl.BlockSpec((None, nb, tt, d), lambda b, t: (layer, b, t, 0))
    ws = w["w_s"][:, :chunk, :chunk]
    bs_t = w["b_s"][:, :chunk].T
    params = [w["norm_g"], w["w_in"], w["ln_v_g"], w["ln_v_b"], ws, bs_t, w["lb_raw"], w["gnorm_g"],
              w["w_pa"], w["w_pb"], w["w_o"], w["final_g"]]

    args = [x, mod]
    in_specs = [tok_spec, pl.BlockSpec((nb, 3, d), lambda b, t: (b, 0, 0))]
    if s0 is not None:
        args.append(s0)
        in_specs.append(st_spec)
    args += params
    in_specs += [full(a) for a in params]
    stacks = [state_stack] + ([v_stack] if emit_v else [])
    aliases = {len(args) + i: 1 + i for i in range(len(stacks))}
    args += stacks
    in_specs += [pl.BlockSpec(memory_space=pl.ANY)] * len(stacks)
    out_shape = [jax.ShapeDtypeStruct(x.shape, F32)] + [jax.ShapeDtypeStruct(a.shape, F32) for a in stacks]
    out_specs = [tok_spec, st_spec] + ([v_spec] if emit_v else [])
    return pl.pallas_call(
        functools.partial(_layer_body, cfg),
        grid=(n_b, n_t),
        in_specs=in_specs,
        out_specs=out_specs,
        out_shape=out_shape,
        input_output_aliases=aliases,
        scratch_shapes=[pltpu.VMEM((nb, HGRN_HEADS, HGRN_DK, HGRN_DV), F32),
                        pltpu.VMEM((rows, d), BF16),
                        pltpu.VMEM((rows // hc, d // LANES, hc, LANES), F32),
                        pltpu.VMEM((rows // hc, hc, d), F32)],
        compiler_params=pltpu.CompilerParams(
            dimension_semantics=("arbitrary", "arbitrary"),
            vmem_limit_bytes=V7X_VMEM_LIMIT_BYTES),
        name=f"trunk_layer{layer}_{'sample' if emit_v else 'prompt'}",
    )(*args)


def _split_bf16(a):
    hi = a.astype(BF16)
    return hi, (a - hi.astype(F32)).astype(BF16)


def _ada_body(c_ref, w_ref, b_ref, o_ref):
    n = c_ref.shape[0]
    s_hi, s_lo = _split_bf16(_silu(c_ref[...]))
    w_hi, w_lo = _split_bf16(w_ref[0])
    both = jnp.dot(jnp.concatenate([s_hi, s_lo], axis=0), w_hi, preferred_element_type=F32)
    o_ref[0] = both[:n] + both[n:] + jnp.dot(s_hi, w_lo, preferred_element_type=F32) + b_ref[0]


def _ada_modulation(c_all, w_ada, b_ada):
    depth, d, d3 = w_ada.shape
    n = c_all.shape[0]
    return pl.pallas_call(
        _ada_body,
        grid=(depth, d3 // d),
        in_specs=[pl.BlockSpec((n, d), lambda l, j: (0, 0)),
                  pl.BlockSpec((1, d, d), lambda l, j: (l, 0, j)),
                  pl.BlockSpec((1, 1, d), lambda l, j: (l, 0, j))],
        out_specs=pl.BlockSpec((1, n, d), lambda l, j: (l, 0, j)),
        out_shape=jax.ShapeDtypeStruct((depth, n, d3), F32),
        compiler_params=pltpu.CompilerParams(dimension_semantics=("arbitrary", "arbitrary")),
        name="ada_modulation",
    )(c_all, w_ada, b_ada.reshape(depth, 1, d3))


def kernel(x_prompt, x_sample, state_hgrn, c_prompt, c_sample, w_ada, b_ada, norm_g, w_in, ln_v_g, ln_v_b,
           w_s, b_s, lb_raw, gnorm_g, w_pa, w_pb, w_o, final_g):
    depth, d = norm_g.shape
    n_prompt = x_prompt.shape[0]
    mod = _ada_modulation(jnp.concatenate([c_prompt, c_sample], axis=0), w_ada, b_ada)
    mod = mod.reshape(depth, -1, 3, d)

    xp, xs = x_prompt, x_sample
    n_sample = x_sample.shape[0]
    sp = jnp.zeros((depth, n_prompt, HGRN_HEADS, HGRN_DK, HGRN_DV), F32)
    ss = jnp.zeros((depth, n_sample, HGRN_HEADS, HGRN_DK, HGRN_DV), F32)
    vs = jnp.zeros((depth,) + x_sample.shape, F32)
    for l in range(depth):
        (w_in_l,) = _pack_rows_bf16([w_in], l)
        w_pa_l, w_pb_l, w_o_l = _pack_rows_bf16([w_pa, w_pb, w_o], l)
        w = dict(norm_g=norm_g[l][None], w_in=w_in_l, ln_v_g=ln_v_g[l][None],
                 ln_v_b=ln_v_b[l][None], w_s=w_s[l], b_s=b_s[l], lb_raw=lb_raw, gnorm_g=gnorm_g[l][None],
                 w_pa=w_pa_l, w_pb=w_pb_l, w_o=w_o_l, final_g=final_g[None])
        final = l == depth - 1
        xp, sp = _trunk_layer(l, xp, mod[l, :n_prompt], None, w, final, sp, None)
        xs, ss, vs = _trunk_layer(l, xs, mod[l, n_prompt:], state_hgrn, w, final, ss, vs)
    return (xp, xs, sp, ss, vs)
```

```python
import functools
import math
from typing import NamedTuple

import jax
import jax.numpy as jnp
from jax import lax
from jax.experimental import pallas as pl
from jax.experimental.pallas import tpu as pltpu

F32 = jnp.float32
BF16 = jnp.bfloat16

EPS = 1e-6
LANES = 128
GMLP_CHUNK = 128
GMLP_GROUPS = 4
HGRN_HEADS = 8
HGRN_DK = 128
HGRN_DV = 128
N_IN_SLABS = 9
SLAB_U, SLAB_V, SLAB_ZA, SLAB_Q, SLAB_F, SLAB_I, SLAB_ZB, SLAB_GA, SLAB_GB = range(N_IN_SLABS)

HGRN_CHUNK = 128
HGRN_SUBBLOCK = 32
SUBLANES = 8
EXP2_CLAMP = 115.0
LOG2E = math.log2(math.e)
TILE_ROWS = 256
PACK_BLOCK_COLS = 1024
PREFETCH_MIN_STEPS = 32
V7X_VMEM_LIMIT_BYTES = 60000 * 1024

_NT = (((1,), (1,)), ((), ()))


class _Cfg(NamedTuple):
    layer: int
    nb: int
    tt: int
    gc: int
    hc: int
    has_s0: bool
    emit_v: bool
    final: bool
    prefetch_h: bool


def _gelu_tanh(x):
    c = math.sqrt(2.0 / math.pi)
    half_x = 0.5 * x
    return half_x + half_x * jnp.tanh(x * (c + (c * 0.044715) * (x * x)))


def _exp_neg(x):
    return jnp.exp2(x * (-LOG2E))


def _sigmoid(x):
    return 1.0 / (1.0 + _exp_neg(x))


def _silu(x):
    return x * _sigmoid(x)


def _rms(x):
    return lax.rsqrt(jnp.mean(x * x, axis=-1, keepdims=True) + EPS)


def _pack_body(*refs):
    n = len(refs) // 2
    for w_ref, o_ref in zip(refs[:n], refs[n:]):
        o_ref[...] = pltpu.bitcast(w_ref[...].astype(BF16), jnp.uint32)


def _pack_rows_bf16(w_stacks, layer):
    _, k, n = w_stacks[0].shape
    assert all(w.shape[1:] == (k, n) for w in w_stacks)
    bn = min(n, PACK_BLOCK_COLS)
    assert n % bn == 0
    return pl.pallas_call(
        _pack_body,
        grid=(n // bn,),
        in_specs=[pl.BlockSpec((None, k, bn), lambda j: (layer, 0, j))] * len(w_stacks),
        out_specs=[pl.BlockSpec((k // 2, bn), lambda j: (0, j))] * len(w_stacks),
        out_shape=[jax.ShapeDtypeStruct((k // 2, n), jnp.uint32)] * len(w_stacks),
        compiler_params=pltpu.CompilerParams(dimension_semantics=("arbitrary",)),
        name="pack_bf16_rows",
    )(*w_stacks)


def _as_bf16(packed):
    return pltpu.bitcast(packed, BF16)


def _row_bcast(a, idx, n_rep):
    return jnp.broadcast_to(a[idx:idx + 1, :], (n_rep, a.shape[1]))


def _forget_lower_bound(lbraw_ref, layer):
    raw = lbraw_ref[...]
    mx = jnp.max(raw, axis=0, keepdims=True)
    ex = jnp.exp(raw - mx)
    den = jnp.sum(ex, axis=0, keepdims=True)
    num = jnp.zeros_like(den)
    for j in range(1, layer + 1):
        num = num + ex[j:j + 1, :]
    return num / den


def _hgrn_prepare(qs, kk, lf2, loc_ref):
    c, d = qs.shape
    dg = HGRN_SUBBLOCK
    nblk = c // dg
    rep = SUBLANES // nblk
    kidx = lax.broadcasted_iota(jnp.int32, (SUBLANES, d), 0)
    slabs = []
    for j in range(c // SUBLANES):
        v = lf2[j * SUBLANES:(j + 1) * SUBLANES]
        sh = 1
        while sh < SUBLANES:
            v = v + jnp.where(kidx >= sh, pltpu.roll(v, sh, 0), 0.0)
            sh *= 2
        if j % (dg // SUBLANES):
            v = v + _row_bcast(slabs[-1], SUBLANES - 1, SUBLANES)
        slabs.append(v)
    loc = jnp.concatenate(slabs, axis=0)
    n_lane_tiles = d // LANES
    for t in range(n_lane_tiles):
        loc_ref[t] = loc[:, t * LANES:(t + 1) * LANES]
    step = dg // rep
    part = jnp.concatenate([loc_ref[t, pl.ds(step - 1, SUBLANES, stride=step), :]
                            for t in range(n_lane_tiles)], axis=1)
    tot = part
    sh = 1
    while sh < rep:
        tot = jnp.where((kidx & (rep - 1)) >= rep - sh, tot, pltpu.roll(tot, SUBLANES - sh, 0))
        sh *= 2
    r_end = tot
    sh = rep
    while sh < SUBLANES:
        r_end = r_end + jnp.where(kidx >= sh, pltpu.roll(r_end, sh, 0), 0.0)
        sh *= 2
    r_start = r_end - tot
    b_last = r_end[SUBLANES - 1:SUBLANES, :]

    tot_rows = jnp.concatenate([_row_bcast(tot, j * rep, dg) for j in range(nblk)], axis=0)
    q_hat = qs * jnp.exp2(loc)
    k_hat = kk * jnp.exp2(tot_rows - loc)
    q_dg = q_hat.astype(BF16)
    k_dg = (kk * jnp.exp2(jnp.minimum(-loc, EXP2_CLAMP))).astype(BF16)

    def scaled_f32(fac, use_q, unit):
        parts = []
        for j in range(nblk):
            rows_j = (q_hat if use_q[j] else k_hat)[j * dg:(j + 1) * dg]
            parts.append(rows_j if unit[j] else rows_j * _row_bcast(fac, j * rep, dg))
        return jnp.concatenate(parts, axis=0)

    def scaled(fac, use_q, unit):
        return scaled_f32(fac, use_q, unit).astype(BF16)

    q_in = scaled(jnp.exp2(r_start), [True] * nblk, [j == 0 for j in range(nblk)])
    k_end_t = scaled_f32(jnp.exp2(b_last - r_end), [False] * nblk,
                         [j == nblk - 1 for j in range(nblk)]).T.astype(BF16)
    decay_t = jnp.broadcast_to(jnp.exp2(b_last), (HGRN_DV, d)).T

    ti = lax.broadcasted_iota(jnp.int32, (c, c), 0)
    si = lax.broadcasted_iota(jnp.int32, (c, c), 1)
    levels = []
    hb = nblk // 2
    while hb >= 1:
        m = None
        for g in reversed(range(nblk // (2 * hb))):
            row = _row_bcast(r_start, (g * 2 * hb + hb) * rep, SUBLANES)
            m = row if m is None else jnp.where(kidx < (g + 1) * 2 * hb * rep, row, m)
        fac = jnp.exp2(jnp.where((kidx & (hb * rep)) != 0, r_start - m, m - r_end))
        unit = [(j % hb == 0) if (j & hb) else (j % hb == hb - 1) for j in range(nblk)]
        x = scaled(fac, [(j & hb) != 0 for j in range(nblk)], unit)
        half = hb * dg
        mask = ((ti ^ si) < 2 * half) & ((ti & half) != 0) & ((si & half) == 0)
        levels.append((x, mask))
        hb //= 2
    worst = jnp.max(jnp.max(-tot, axis=1, keepdims=True), axis=0, keepdims=True)
    mask_dg = ((ti ^ si) < dg) & (si <= ti) & (worst <= EXP2_CLAMP)
    return dict(q_in=q_in, k_end_t=k_end_t, decay_t=decay_t, levels=levels, q_dg=q_dg, k_dg=k_dg,
                mask_dg=mask_dg, flagged=worst[0, 0] > EXP2_CLAMP, loc=loc, qs=qs, kk=kk)


def _hgrn_intra(p):
    out = []
    for h in range(HGRN_HEADS):
        hs = slice(h * HGRN_DK, (h + 1) * HGRN_DK)
        a = jnp.where(p["mask_dg"],
                      lax.dot_general(p["q_dg"][:, hs], p["k_dg"][:, hs], _NT, preferred_element_type=F32), 0.0)
        for x, mask in p["levels"]:
            xh = x[:, hs]
            a = jnp.where(mask, lax.dot_general(xh, xh, _NT, preferred_element_type=F32), a)
        out.append(a.astype(BF16))
    return out


def _hgrn_apply(p, a_heads, iv, st_ref, n, o_ref):
    c = iv.shape[0]
    for h in range(HGRN_HEADS):
        hs = slice(h * HGRN_DK, (h + 1) * HGRN_DK)
        vs = slice(h * HGRN_DV, (h + 1) * HGRN_DV)
        st = st_ref[n, h]
        stacked = jnp.dot(jnp.concatenate([a_heads[h], p["k_end_t"][hs, :]], axis=0), iv[:, vs],
                          preferred_element_type=F32)
        o_ref[:, vs] = jnp.dot(p["q_in"][:, hs], st.astype(BF16), preferred_element_type=F32) + stacked[:c]
        st_ref[n, h] = st * p["decay_t"][hs, :] + stacked[c:]


def _hgrn_exact_diag(p, iv):
    qs, kk, loc = p["qs"], p["kk"], p["loc"]
    c, d = qs.shape
    dg = HGRN_SUBBLOCK
    ivf = iv.astype(F32)
    inblk = lax.broadcasted_iota(jnp.int32, (c, d), 0) & (dg - 1)

    def body(dd, acc):
        valid = inblk >= dd
        expo = jnp.where(valid, loc - pltpu.roll(loc, dd, 0), 0.0)
        w = jnp.where(valid, qs * pltpu.roll(kk, dd, 0) * jnp.exp2(expo), 0.0)
        ir = pltpu.roll(ivf, dd, 0)
        parts = []
        for h in range(HGRN_HEADS):
            hs = slice(h * HGRN_DK, (h + 1) * HGRN_DK)
            vs = slice(h * HGRN_DV, (h + 1) * HGRN_DV)
            parts.append(jnp.sum(w[:, hs], axis=-1, keepdims=True) * ir[:, vs])
        return acc + jnp.concatenate(parts, axis=1)

    return lax.fori_loop(0, dg, body, jnp.zeros((c, d), F32))


def _layer_body(cfg, *refs):
    refs = list(refs)
    x_ref, mod_ref = refs[:2]
    pos = 2
    xn_ref = modn_ref = None
    if cfg.prefetch_h:
        xn_ref, modn_ref = refs[pos:pos + 2]
        pos += 2
    s0_ref = None
    if cfg.has_s0:
        s0_ref = refs[pos]
        pos += 1
    (normg_ref, win_ref, lnvg_ref, lnvb_ref, ws_ref, bs_ref, lbraw_ref, gng_ref,
     wpa_ref, wpb_ref, wo_ref, fing_ref) = refs[pos:pos + 12]
    pos += 12
    pos += 2 if cfg.emit_v else 1
    y_ref, sout_ref = refs[pos:pos + 2]
    pos += 2
    v_ref = None
    if cfg.emit_v:
        v_ref = refs[pos]
        pos += 1
    st_ref, yb_ref, loc_ref, o_ref = refs[pos:pos + 4]
    h_ref = refs[pos + 4] if cfg.prefetch_h else None

    nb, tt = cfg.nb, cfg.tt
    rows = nb * tt
    d = x_ref.shape[-1]
    t_idx = pl.program_id(1)
    n_t = pl.num_programs(1)

    def modulated_norm(xr, mr):
        parts = []
        for n in range(nb):
            x = xr[n]
            shift = mr[n, 0:1, :]
            scale = mr[n, 1:2, :]
            parts.append(((x * _rms(x)) * (normg_ref[...] * (1.0 + scale)) + shift).astype(BF16))
        return parts[0] if nb == 1 else jnp.concatenate(parts, axis=0)

    if cfg.prefetch_h:
        step_idx = pl.program_id(0) * n_t + t_idx
        slot = step_idx % 2

        @pl.when(step_idx == 0)
        def _():
            h_ref[0] = modulated_norm(x_ref, mod_ref)

    @pl.when(t_idx == 0)
    def _():
        for n in range(nb):
            for hd in range(HGRN_HEADS):
                if cfg.has_s0:
                    st_ref[n, hd] = s0_ref[n, hd]
                else:
                    st_ref[n, hd] = jnp.zeros((HGRN_DK, HGRN_DV), F32)

    if cfg.prefetch_h:
        h = h_ref[slot]
        h_ref[1 - slot] = modulated_norm(xn_ref, modn_ref)
    else:
        h = modulated_norm(x_ref, mod_ref)

    def proj(slab):
        return jnp.dot(h, _as_bf16(win_ref[:, slab * d:(slab + 1) * d]), preferred_element_type=F32)

    z = proj(SLAB_F)
    pq = proj(SLAB_Q)
    iv = proj(SLAB_I).astype(BF16)
    pv = proj(SLAB_V)
    pu = proj(SLAB_U)
    pza = proj(SLAB_ZA)

    t_neg = jnp.abs(z) * (-LOG2E)
    e = jnp.exp2(t_neg)
    inv = 1.0 / (1.0 + e)
    log2_sig = jnp.where(z >= 0.0, 0.0, t_neg) + jnp.log2(inv)
    sig_neg = jnp.where(z >= 0.0, e, 1.0) * inv
    if cfg.layer == 0:
        lf2 = log2_sig
        kk = sig_neg
    else:
        lb = _forget_lower_bound(lbraw_ref, cfg.layer)
        f = lb + (1.0 - lb) * (jnp.where(z >= 0.0, 1.0, e) * inv)
        lf2 = jnp.where(f > 0.0, jnp.log2(f), log2_sig)
        kk = (1.0 - lb) * sig_neg
    qs = _silu(pq)

    v = _gelu_tanh(pv)
    mu = jnp.mean(v, axis=-1, keepdims=True)
    vc = v - mu
    var = jnp.mean(vc * vc, axis=-1, keepdims=True)
    vn = vc * lax.rsqrt(var + EPS) * lnvg_ref[...] + lnvb_ref[...]
    if cfg.emit_v:
        for n in range(nb):
            v_ref[n] = vn[n * tt:(n + 1) * tt]
    vb = vn.astype(BF16)
    gc = cfg.gc
    gdim = d // GMLP_GROUPS
    tril = (lax.broadcasted_iota(jnp.int32, (gc, gc), 0) >= lax.broadcasted_iota(jnp.int32, (gc, gc), 1))
    w_tril = [jnp.where(tril, ws_ref[g], 0.0).astype(BF16) for g in range(GMLP_GROUPS)]
    s_chunks = []
    for j in range(rows // gc):
        parts = [jnp.dot(w_tril[g], vb[j * gc:(j + 1) * gc, g * gdim:(g + 1) * gdim],
                         preferred_element_type=F32) + bs_ref[:, g:g + 1]
                 for g in range(GMLP_GROUPS)]
        s_chunks.append(jnp.concatenate(parts, axis=1))
    s = s_chunks[0] if len(s_chunks) == 1 else jnp.concatenate(s_chunks, axis=0)

    pzb = proj(SLAB_ZB)
    pga = proj(SLAB_GA)

    hc = cfg.hc
    cps = tt // hc
    n_chunks = rows // hc
    prep = [_hgrn_prepare(qs[j * hc:(j + 1) * hc], kk[j * hc:(j + 1) * hc], lf2[j * hc:(j + 1) * hc],
                          loc_ref.at[j])
            for j in range(n_chunks)]
    a_heads = [_hgrn_intra(p) for p in prep]

    ya = (_gelu_tanh(pu) * s * _silu(pza)).astype(BF16)
    m_a = _sigmoid(pga) * jnp.dot(ya, _as_bf16(wpa_ref[...]), preferred_element_type=F32)

    gate_b = _silu(pzb)
    gng = gng_ref[...]
    for j in range(n_chunks):
        _hgrn_apply(prep[j], a_heads[j], iv[j * hc:(j + 1) * hc], st_ref, j // cps, o_ref.at[j])
    sig_gb = _sigmoid(proj(SLAB_GB))

    def finish():
        for j in range(n_chunks):
            for hd in range(HGRN_HEADS):
                vs = slice(hd * HGRN_DV, (hd + 1) * HGRN_DV)
                o = o_ref[j, :, vs]
                yb_ref[j * hc:(j + 1) * hc, vs] = (o * _rms(o) * gng[:, vs]
                                                   * gate_b[j * hc:(j + 1) * hc, vs]).astype(BF16)
        m_b = sig_gb * jnp.dot(yb_ref[...], _as_bf16(wpb_ref[...]), preferred_element_type=F32)
        upd = jnp.dot((m_a + m_b).astype(BF16), _as_bf16(wo_ref[...]), preferred_element_type=F32)
        for n in range(nb):
            gate = mod_ref[n, 2:3, :]
            xo = x_ref[n] + gate * upd[n * tt:(n + 1) * tt]
            if cfg.final:
                xo = (xo * _rms(xo)) * fing_ref[...]
            y_ref[n] = xo

    finish()

    any_flagged = prep[0]["flagged"]
    for p in prep[1:]:
        any_flagged = jnp.logical_or(any_flagged, p["flagged"])

    @pl.when(any_flagged)
    def _():
        for j in range(n_chunks):
            @pl.when(prep[j]["flagged"])
            def _():
                o_ref[j] = o_ref[j] + _hgrn_exact_diag(prep[j], iv[j * hc:(j + 1) * hc])
        finish()

    @pl.when(t_idx == n_t - 1)
    def _():
        for n in range(nb):
            for hd in range(HGRN_HEADS):
                sout_ref[n, hd] = st_ref[n, hd]


def _tiling(batch, seq):
    chunk = min(seq, GMLP_CHUNK)
    assert seq % chunk == 0 and chunk % HGRN_SUBBLOCK == 0 and SUBLANES % (chunk // HGRN_SUBBLOCK) == 0
    if seq >= TILE_ROWS:
        assert seq % TILE_ROWS == 0
        return 1, TILE_ROWS, chunk
    nb = max(1, TILE_ROWS // seq)
    while batch % nb:
        nb -= 1
    return nb, seq, chunk


def _trunk_layer(layer, x, mod, s0, w, final, state_stack, v_stack):
    emit_v = v_stack is not None
    batch, seq, d = x.shape
    nb, tt, chunk = _tiling(batch, seq)
    hc = min(chunk, HGRN_CHUNK)
    n_b, n_t = batch // nb, seq // tt
    cfg = _Cfg(layer=layer, nb=nb, tt=tt, gc=chunk, hc=hc,
               has_s0=s0 is not None, emit_v=emit_v, final=final,
               prefetch_h=n_b * n_t >= PREFETCH_MIN_STEPS)
    rows = nb * tt

    def full(a):
        nd = a.ndim
        return pl.BlockSpec(a.shape, lambda b, t, _nd=nd: (0,) * _nd)

    tok_spec = pl.BlockSpec((nb, tt, d), lambda b, t: (b, t, 0))
    st_spec = pl.BlockSpec((None, nb, HGRN_HEADS, HGRN_DK, HGRN_DV), lambda b, t: (layer, b, 0, 0, 0))
    v_spec = pl.BlockSpec((None, nb, tt, d), lambda b, t: (layer, b, t, 0))
    ws = w["w_s"][:, :chunk, :chunk]
    bs_t = w["b_s"][:, :chunk].T
    params = [w["norm_g"], w["w_in"], w["ln_v_g"], w["ln_v_b"], ws, bs_t, w["lb_raw"], w["gnorm_g"],
              w["w_pa"], w["w_pb"], w["w_o"], w["final_g"]]

    def next_step(b, t):
        f = jnp.minimum(b * n_t + t + 1, n_b * n_t - 1)
        return f // n_t, f % n_t

    def next_tok(b, t):
        bn, tn = next_step(b, t)
        return bn, tn, 0

    def next_mod(b, t):
        return next_step(b, t)[0], 0, 0

    args = [x, mod]
    in_specs = [tok_spec, pl.BlockSpec((nb, 3, d), lambda b, t: (b, 0, 0))]
    scratch = [pltpu.VMEM((nb, HGRN_HEADS, HGRN_DK, HGRN_DV), F32),
               pltpu.VMEM((rows, d), BF16),
               pltpu.VMEM((rows // hc, d // LANES, hc, LANES), F32),
               pltpu.VMEM((rows // hc, hc, d), F32)]
    if cfg.prefetch_h:
        args += [x, mod]
        in_specs += [pl.BlockSpec((nb, tt, d), next_tok), pl.BlockSpec((nb, 3, d), next_mod)]
        scratch.append(pltpu.VMEM((2, rows, d), BF16))
    if s0 is not None:
        args.append(s0)
        in_specs.append(st_spec)
    args += params
    in_specs += [full(a) for a in params]
    stacks = [state_stack] + ([v_stack] if emit_v else [])
    aliases = {len(args) + i: 1 + i for i in range(len(stacks))}
    args += stacks
    in_specs += [pl.BlockSpec(memory_space=pl.ANY)] * len(stacks)
    out_shape = [jax.ShapeDtypeStruct(x.shape, F32)] + [jax.ShapeDtypeStruct(a.shape, F32) for a in stacks]
    out_specs = [tok_spec, st_spec] + ([v_spec] if emit_v else [])
    return pl.pallas_call(
        functools.partial(_layer_body, cfg),
        grid=(n_b, n_t),
        in_specs=in_specs,
        out_specs=out_specs,
        out_shape=out_shape,
        input_output_aliases=aliases,
        scratch_shapes=scratch,
        compiler_params=pltpu.CompilerParams(
            dimension_semantics=("arbitrary", "arbitrary"),
            vmem_limit_bytes=V7X_VMEM_LIMIT_BYTES),
        name=f"trunk_layer{layer}_{'sample' if emit_v else 'prompt'}",
    )(*args)


def _split_bf16(a):
    hi = a.astype(BF16)
    return hi, (a - hi.astype(F32)).astype(BF16)


def _ada_body(c_ref, w_ref, b_ref, o_ref):
    n = c_ref.shape[0]
    s_hi, s_lo = _split_bf16(_silu(c_ref[...]))
    w_hi, w_lo = _split_bf16(w_ref[0])
    both = jnp.dot(jnp.concatenate([s_hi, s_lo], axis=0), w_hi, preferred_element_type=F32)
    o_ref[0] = both[:n] + both[n:] + jnp.dot(s_hi, w_lo, preferred_element_type=F32) + b_ref[0]


def _ada_modulation(c_all, w_ada, b_ada):
    depth, d, d3 = w_ada.shape
    n = c_all.shape[0]
    return pl.pallas_call(
        _ada_body,
        grid=(depth, d3 // d),
        in_specs=[pl.BlockSpec((n, d), lambda l, j: (0, 0)),
                  pl.BlockSpec((1, d, d), lambda l, j: (l, 0, j)),
                  pl.BlockSpec((1, 1, d), lambda l, j: (l, 0, j))],
        out_specs=pl.BlockSpec((1, n, d), lambda l, j: (l, 0, j)),
        out_shape=jax.ShapeDtypeStruct((depth, n, d3), F32),
        compiler_params=pltpu.CompilerParams(dimension_semantics=("arbitrary", "arbitrary")),
        name="ada_modulation",
    )(c_all, w_ada, b_ada.reshape(depth, 1, d3))


def kernel(x_prompt, x_sample, state_hgrn, c_prompt, c_sample, w_ada, b_ada, norm_g, w_in, ln_v_g, ln_v_b,
           w_s, b_s, lb_raw, gnorm_g, w_pa, w_pb, w_o, final_g):
    depth, d = norm_g.shape
    n_prompt = x_prompt.shape[0]
    mod = _ada_modulation(jnp.concatenate([c_prompt, c_sample], axis=0), w_ada, b_ada)
    mod = mod.reshape(depth, -1, 3, d)

    xp, xs = x_prompt, x_sample
    n_sample = x_sample.shape[0]
    sp = jnp.zeros((depth, n_prompt, HGRN_HEADS, HGRN_DK, HGRN_DV), F32)
    ss = jnp.zeros((depth, n_sample, HGRN_HEADS, HGRN_DK, HGRN_DV), F32)
    vs = jnp.zeros((depth,) + x_sample.shape, F32)
    for l in range(depth):
        (w_in_l,) = _pack_rows_bf16([w_in], l)
        w_pa_l, w_pb_l, w_o_l = _pack_rows_bf16([w_pa, w_pb, w_o], l)
        w = dict(norm_g=norm_g[l][None], w_in=w_in_l, ln_v_g=ln_v_g[l][None],
                 ln_v_b=ln_v_b[l][None], w_s=w_s[l], b_s=b_s[l], lb_raw=lb_raw, gnorm_g=gnorm_g[l][None],
                 w_pa=w_pa_l, w_pb=w_pb_l, w_o=w_o_l, final_g=final_g[None])
        final = l == depth - 1
        xp, sp = _trunk_layer(l, xp, mod[l, :n_prompt], None, w, final, sp, None)
        xs, ss, vs = _trunk_layer(l, xs, mod[l, n_prompt:], state_hgrn, w, final, ss, vs)
    return (xp, xs, sp, ss, vs)
```

```python
import functools
import math
from typing import NamedTuple

import jax
import jax.numpy as jnp
from jax import lax
from jax.experimental import pallas as pl
from jax.experimental.pallas import tpu as pltpu

F32 = jnp.float32
BF16 = jnp.bfloat16

EPS = 1e-6
LANES = 128
GMLP_CHUNK = 128
GMLP_GROUPS = 4
HGRN_HEADS = 8
HGRN_DK = 128
HGRN_DV = 128
N_IN_SLABS = 9
SLAB_U, SLAB_V, SLAB_ZA, SLAB_Q, SLAB_F, SLAB_I, SLAB_ZB, SLAB_GA, SLAB_GB = range(N_IN_SLABS)

HGRN_CHUNK = 128
HGRN_SUBBLOCK = 32
SUBLANES = 8
EXP2_CLAMP = 115.0
LOG2E = math.log2(math.e)
TILE_ROWS = 256
PACK_BLOCK_COLS = 1024
V7X_VMEM_LIMIT_BYTES = 60000 * 1024

_NT = (((1,), (1,)), ((), ()))


class _Cfg(NamedTuple):
    layer: int
    nb: int
    tt: int
    gc: int
    hc: int
    has_s0: bool
    emit_v: bool
    final: bool


def _gelu_tanh(x):
    c = math.sqrt(2.0 / math.pi)
    half_x = 0.5 * x
    return half_x + half_x * jnp.tanh(x * (c + (c * 0.044715) * (x * x)))


def _exp_neg(x):
    return jnp.exp2(x * (-LOG2E))


def _sigmoid(x):
    return 1.0 / (1.0 + _exp_neg(x))


def _silu(x):
    return x * _sigmoid(x)


def _rms(x):
    return lax.rsqrt(jnp.mean(x * x, axis=-1, keepdims=True) + EPS)


def _pack_body(*refs):
    n = len(refs) // 2
    for w_ref, o_ref in zip(refs[:n], refs[n:]):
        o_ref[...] = pltpu.bitcast(w_ref[...].astype(BF16), jnp.uint32)


def _pack_rows_bf16(w_stacks, layer):
    _, k, n = w_stacks[0].shape
    assert all(w.shape[1:] == (k, n) for w in w_stacks)
    bn = min(n, PACK_BLOCK_COLS)
    assert n % bn == 0
    return pl.pallas_call(
        _pack_body,
        grid=(n // bn,),
        in_specs=[pl.BlockSpec((None, k, bn), lambda j: (layer, 0, j))] * len(w_stacks),
        out_specs=[pl.BlockSpec((k // 2, bn), lambda j: (0, j))] * len(w_stacks),
        out_shape=[jax.ShapeDtypeStruct((k // 2, n), jnp.uint32)] * len(w_stacks),
        compiler_params=pltpu.CompilerParams(dimension_semantics=("arbitrary",)),
        name="pack_bf16_rows",
    )(*w_stacks)


def _as_bf16(packed):
    return pltpu.bitcast(packed, BF16)


def _row_bcast(a, idx, n_rep):
    return jnp.broadcast_to(a[idx:idx + 1, :], (n_rep, a.shape[1]))


def _forget_lower_bound(lbraw_ref, layer):
    raw = lbraw_ref[...]
    mx = jnp.max(raw, axis=0, keepdims=True)
    ex = jnp.exp(raw - mx)
    den = jnp.sum(ex, axis=0, keepdims=True)
    num = jnp.zeros_like(den)
    for j in range(1, layer + 1):
        num = num + ex[j:j + 1, :]
    return num / den


def _hgrn_prepare(qs, kk, lf2, loc_ref):
    c, d = qs.shape
    dg = HGRN_SUBBLOCK
    nblk = c // dg
    rep = SUBLANES // nblk
    kidx = lax.broadcasted_iota(jnp.int32, (SUBLANES, d), 0)
    slabs = []
    for j in range(c // SUBLANES):
        v = lf2[j * SUBLANES:(j + 1) * SUBLANES]
        sh = 1
        while sh < SUBLANES:
            v = v + jnp.where(kidx >= sh, pltpu.roll(v, sh, 0), 0.0)
            sh *= 2
        if j % (dg // SUBLANES):
            v = v + _row_bcast(slabs[-1], SUBLANES - 1, SUBLANES)
        slabs.append(v)
    loc = jnp.concatenate(slabs, axis=0)
    n_lane_tiles = d // LANES
    for t in range(n_lane_tiles):
        loc_ref[t] = loc[:, t * LANES:(t + 1) * LANES]
    step = dg // rep
    part = jnp.concatenate([loc_ref[t, pl.ds(step - 1, SUBLANES, stride=step), :]
                            for t in range(n_lane_tiles)], axis=1)
    tot = part
    sh = 1
    while sh < rep:
        tot = jnp.where((kidx & (rep - 1)) >= rep - sh, tot, pltpu.roll(tot, SUBLANES - sh, 0))
        sh *= 2
    r_end = tot
    sh = rep
    while sh < SUBLANES:
        r_end = r_end + jnp.where(kidx >= sh, pltpu.roll(r_end, sh, 0), 0.0)
        sh *= 2
    r_start = r_end - tot
    b_last = r_end[SUBLANES - 1:SUBLANES, :]

    tot_rows = jnp.concatenate([_row_bcast(tot, j * rep, dg) for j in range(nblk)], axis=0)
    q_hat = qs * jnp.exp2(loc)
    k_hat = kk * jnp.exp2(tot_rows - loc)
    q_dg = q_hat.astype(BF16)
    k_dg = (kk * jnp.exp2(jnp.minimum(-loc, EXP2_CLAMP))).astype(BF16)

    def scaled_f32(fac, use_q, unit):
        parts = []
        for j in range(nblk):
            rows_j = (q_hat if use_q[j] else k_hat)[j * dg:(j + 1) * dg]
            parts.append(rows_j if unit[j] else rows_j * _row_bcast(fac, j * rep, dg))
        return jnp.concatenate(parts, axis=0)

    def scaled(fac, use_q, unit):
        return scaled_f32(fac, use_q, unit).astype(BF16)

    q_in = scaled(jnp.exp2(r_start), [True] * nblk, [j == 0 for j in range(nblk)])
    k_end_t = scaled_f32(jnp.exp2(b_last - r_end), [False] * nblk,
                         [j == nblk - 1 for j in range(nblk)]).T.astype(BF16)
    decay_t = jnp.broadcast_to(jnp.exp2(b_last), (HGRN_DV, d)).T

    ti = lax.broadcasted_iota(jnp.int32, (c, c), 0)
    si = lax.broadcasted_iota(jnp.int32, (c, c), 1)
    levels = []
    hb = nblk // 2
    while hb >= 1:
        m = None
        for g in reversed(range(nblk // (2 * hb))):
            row = _row_bcast(r_start, (g * 2 * hb + hb) * rep, SUBLANES)
            m = row if m is None else jnp.where(kidx < (g + 1) * 2 * hb * rep, row, m)
        fac = jnp.exp2(jnp.where((kidx & (hb * rep)) != 0, r_start - m, m - r_end))
        unit = [(j % hb == 0) if (j & hb) else (j % hb == hb - 1) for j in range(nblk)]
        x = scaled(fac, [(j & hb) != 0 for j in range(nblk)], unit)
        half = hb * dg
        mask = ((ti ^ si) < 2 * half) & ((ti & half) != 0) & ((si & half) == 0)
        levels.append((x, mask))
        hb //= 2
    worst = jnp.max(jnp.max(-tot, axis=1, keepdims=True), axis=0, keepdims=True)
    mask_dg = ((ti ^ si) < dg) & (si <= ti) & (worst <= EXP2_CLAMP)
    return dict(q_in=q_in, k_end_t=k_end_t, decay_t=decay_t, levels=levels, q_dg=q_dg, k_dg=k_dg,
                mask_dg=mask_dg, flagged=worst[0, 0] > EXP2_CLAMP, loc=loc, qs=qs, kk=kk)


def _hgrn_intra(p):
    out = []
    for h in range(HGRN_HEADS):
        hs = slice(h * HGRN_DK, (h + 1) * HGRN_DK)
        a = jnp.where(p["mask_dg"],
                      lax.dot_general(p["q_dg"][:, hs], p["k_dg"][:, hs], _NT, preferred_element_type=F32), 0.0)
        for x, mask in p["levels"]:
            xh = x[:, hs]
            a = jnp.where(mask, lax.dot_general(xh, xh, _NT, preferred_element_type=F32), a)
        out.append(a.astype(BF16))
    return out


def _hgrn_apply(p, a_heads, iv, st_ref, n, o_ref):
    c = iv.shape[0]
    for h in range(HGRN_HEADS):
        hs = slice(h * HGRN_DK, (h + 1) * HGRN_DK)
        vs = slice(h * HGRN_DV, (h + 1) * HGRN_DV)
        st = st_ref[n, h]
        stacked = jnp.dot(jnp.concatenate([a_heads[h], p["k_end_t"][hs, :]], axis=0), iv[:, vs],
                          preferred_element_type=F32)
        o_ref[:, vs] = jnp.dot(p["q_in"][:, hs], st.astype(BF16), preferred_element_type=F32) + stacked[:c]
        st_ref[n, h] = st * p["decay_t"][hs, :] + stacked[c:]


def _hgrn_exact_diag(p, iv):
    qs, kk, loc = p["qs"], p["kk"], p["loc"]
    c, d = qs.shape
    dg = HGRN_SUBBLOCK
    ivf = iv.astype(F32)
    inblk = lax.broadcasted_iota(jnp.int32, (c, d), 0) & (dg - 1)

    def body(dd, acc):
        valid = inblk >= dd
        expo = jnp.where(valid, loc - pltpu.roll(loc, dd, 0), 0.0)
        w = jnp.where(valid, qs * pltpu.roll(kk, dd, 0) * jnp.exp2(expo), 0.0)
        ir = pltpu.roll(ivf, dd, 0)
        parts = []
        for h in range(HGRN_HEADS):
            hs = slice(h * HGRN_DK, (h + 1) * HGRN_DK)
            vs = slice(h * HGRN_DV, (h + 1) * HGRN_DV)
            parts.append(jnp.sum(w[:, hs], axis=-1, keepdims=True) * ir[:, vs])
        return acc + jnp.concatenate(parts, axis=1)

    return lax.fori_loop(0, dg, body, jnp.zeros((c, d), F32))


def _layer_body(cfg, *refs):
    refs = list(refs)
    x_ref, mod_ref = refs[:2]
    pos = 2
    s0_ref = None
    if cfg.has_s0:
        s0_ref = refs[pos]
        pos += 1
    (normg_ref, win_ref, lnvg_ref, lnvb_ref, ws_ref, bs_ref, lbraw_ref, gng_ref,
     wpa_ref, wpb_ref, wo_ref, fing_ref) = refs[pos:pos + 12]
    pos += 12
    pos += 2 if cfg.emit_v else 1
    y_ref, sout_ref = refs[pos:pos + 2]
    pos += 2
    v_ref = None
    if cfg.emit_v:
        v_ref = refs[pos]
        pos += 1
    st_ref, yb_ref, loc_ref, o_ref = refs[pos:pos + 4]

    nb, tt = cfg.nb, cfg.tt
    rows = nb * tt
    d = x_ref.shape[-1]
    t_idx = pl.program_id(1)
    n_t = pl.num_programs(1)

    @pl.when(t_idx == 0)
    def _():
        for n in range(nb):
            for hd in range(HGRN_HEADS):
                if cfg.has_s0:
                    st_ref[n, hd] = s0_ref[n, hd]
                else:
                    st_ref[n, hd] = jnp.zeros((HGRN_DK, HGRN_DV), F32)

    hs = []
    for n in range(nb):
        x = x_ref[n]
        shift = mod_ref[n, 0:1, :]
        scale = mod_ref[n, 1:2, :]
        hs.append(((x * _rms(x)) * (normg_ref[...] * (1.0 + scale)) + shift).astype(BF16))
    h = hs[0] if nb == 1 else jnp.concatenate(hs, axis=0)

    def proj(slab):
        return jnp.dot(h, _as_bf16(win_ref[:, slab * d:(slab + 1) * d]), preferred_element_type=F32)

    z = proj(SLAB_F)
    pq = proj(SLAB_Q)
    iv = proj(SLAB_I).astype(BF16)
    pv = proj(SLAB_V)
    pu = proj(SLAB_U)
    pza = proj(SLAB_ZA)

    t_neg = jnp.abs(z) * (-LOG2E)
    e = jnp.exp2(t_neg)
    inv = 1.0 / (1.0 + e)
    log2_sig = jnp.where(z >= 0.0, 0.0, t_neg) + jnp.log2(inv)
    sig_neg = jnp.where(z >= 0.0, e, 1.0) * inv
    if cfg.layer == 0:
        lf2 = log2_sig
        kk = sig_neg
    else:
        lb = _forget_lower_bound(lbraw_ref, cfg.layer)
        f = lb + (1.0 - lb) * (jnp.where(z >= 0.0, 1.0, e) * inv)
        lf2 = jnp.where(f > 0.0, jnp.log2(f), log2_sig)
        kk = (1.0 - lb) * sig_neg
    qs = _silu(pq)

    v = _gelu_tanh(pv)
    mu = jnp.mean(v, axis=-1, keepdims=True)
    vc = v - mu
    var = jnp.mean(vc * vc, axis=-1, keepdims=True)
    vn = vc * lax.rsqrt(var + EPS) * lnvg_ref[...] + lnvb_ref[...]
    if cfg.emit_v:
        for n in range(nb):
            v_ref[n] = vn[n * tt:(n + 1) * tt]
    vb = vn.astype(BF16)
    gc = cfg.gc
    gdim = d // GMLP_GROUPS
    tril = (lax.broadcasted_iota(jnp.int32, (gc, gc), 0) >= lax.broadcasted_iota(jnp.int32, (gc, gc), 1))
    w_tril = [jnp.where(tril, ws_ref[g], 0.0).astype(BF16) for g in range(GMLP_GROUPS)]
    s_chunks = []
    for j in range(rows // gc):
        parts = [jnp.dot(w_tril[g], vb[j * gc:(j + 1) * gc, g * gdim:(g + 1) * gdim],
                         preferred_element_type=F32) + bs_ref[:, g:g + 1]
                 for g in range(GMLP_GROUPS)]
        s_chunks.append(jnp.concatenate(parts, axis=1))
    s = s_chunks[0] if len(s_chunks) == 1 else jnp.concatenate(s_chunks, axis=0)

    pzb = proj(SLAB_ZB)
    pga = proj(SLAB_GA)

    hc = cfg.hc
    cps = tt // hc
    n_chunks = rows // hc
    prep = [_hgrn_prepare(qs[j * hc:(j + 1) * hc], kk[j * hc:(j + 1) * hc], lf2[j * hc:(j + 1) * hc],
                          loc_ref.at[j])
            for j in range(n_chunks)]
    a_heads = [_hgrn_intra(p) for p in prep]

    ya = (_gelu_tanh(pu) * s * _silu(pza)).astype(BF16)
    m_a = _sigmoid(pga) * jnp.dot(ya, _as_bf16(wpa_ref[...]), preferred_element_type=F32)

    gate_b = _silu(pzb)
    gng = gng_ref[...]
    for j in range(n_chunks):
        _hgrn_apply(prep[j], a_heads[j], iv[j * hc:(j + 1) * hc], st_ref, j // cps, o_ref.at[j])
    sig_gb = _sigmoid(proj(SLAB_GB))

    def finish():
        for j in range(n_chunks):
            for hd in range(HGRN_HEADS):
                vs = slice(hd * HGRN_DV, (hd + 1) * HGRN_DV)
                o = o_ref[j, :, vs]
                yb_ref[j * hc:(j + 1) * hc, vs] = (o * _rms(o) * gng[:, vs]
                                                   * gate_b[j * hc:(j + 1) * hc, vs]).astype(BF16)
        m_b = sig_gb * jnp.dot(yb_ref[...], _as_bf16(wpb_ref[...]), preferred_element_type=F32)
        upd = jnp.dot((m_a + m_b).astype(BF16), _as_bf16(wo_ref[...]), preferred_element_type=F32)
        for n in range(nb):
            gate = mod_ref[n, 2:3, :]
            xo = x_ref[n] + gate * upd[n * tt:(n + 1) * tt]
            if cfg.final:
                xo = (xo * _rms(xo)) * fing_ref[...]
            y_ref[n] = xo

    finish()

    any_flagged = prep[0]["flagged"]
    for p in prep[1:]:
        any_flagged = jnp.logical_or(any_flagged, p["flagged"])

    @pl.when(any_flagged)
    def _():
        for j in range(n_chunks):
            @pl.when(prep[j]["flagged"])
            def _():
                o_ref[j] = o_ref[j] + _hgrn_exact_diag(prep[j], iv[j * hc:(j + 1) * hc])
        finish()

    @pl.when(t_idx == n_t - 1)
    def _():
        for n in range(nb):
            for hd in range(HGRN_HEADS):
                sout_ref[n, hd] = st_ref[n, hd]


def _tiling(batch, seq):
    chunk = min(seq, GMLP_CHUNK)
    assert seq % chunk == 0 and chunk % HGRN_SUBBLOCK == 0 and SUBLANES % (chunk // HGRN_SUBBLOCK) == 0
    if seq >= TILE_ROWS and batch % (TILE_ROWS // chunk):
        assert seq % TILE_ROWS == 0
        return 1, TILE_ROWS, chunk
    seq = min(seq, chunk)
    nb = max(1, TILE_ROWS // seq)
    while batch % nb:
        nb -= 1
    return nb, seq, chunk


def _trunk_layer(layer, x, mod, s0, w, final, state_stack, v_stack):
    emit_v = v_stack is not None
    batch, seq, d = x.shape
    nb, tt, chunk = _tiling(batch, seq)
    hc = min(chunk, HGRN_CHUNK)
    n_b, n_t = batch // nb, seq // tt
    cfg = _Cfg(layer=layer, nb=nb, tt=tt, gc=chunk, hc=hc,
               has_s0=s0 is not None, emit_v=emit_v, final=final)
    rows = nb * tt

    def full(a):
        nd = a.ndim
        return pl.BlockSpec(a.shape, lambda b, t, _nd=nd: (0,) * _nd)

    tok_spec = pl.BlockSpec((nb, tt, d), lambda b, t: (b, t, 0))
    st_spec = pl.BlockSpec((None, nb, HGRN_HEADS, HGRN_DK, HGRN_DV), lambda b, t: (layer, b, 0, 0, 0))
    v_spec = pl.BlockSpec((None, nb, tt, d), lambda b, t: (layer, b, t, 0))
    ws = w["w_s"][:, :chunk, :chunk]
    bs_t = w["b_s"][:, :chunk].T
    params = [w["norm_g"], w["w_in"], w["ln_v_g"], w["ln_v_b"], ws, bs_t, w["lb_raw"], w["gnorm_g"],
              w["w_pa"], w["w_pb"], w["w_o"], w["final_g"]]

    args = [x, mod]
    in_specs = [tok_spec, pl.BlockSpec((nb, 3, d), lambda b, t: (b, 0, 0))]
    if s0 is not None:
        args.append(s0)
        in_specs.append(st_spec)
    args += params
    in_specs += [full(a) for a in params]
    stacks = [state_stack] + ([v_stack] if emit_v else [])
    aliases = {len(args) + i: 1 + i for i in range(len(stacks))}
    args += stacks
    in_specs += [pl.BlockSpec(memory_space=pl.ANY)] * len(stacks)
    out_shape = [jax.ShapeDtypeStruct(x.shape, F32)] + [jax.ShapeDtypeStruct(a.shape, F32) for a in stacks]
    out_specs = [tok_spec, st_spec] + ([v_spec] if emit_v else [])
    return pl.pallas_call(
        functools.partial(_layer_body, cfg),
        grid=(n_b, n_t),
        in_specs=in_specs,
        out_specs=out_specs,
        out_shape=out_shape,
        input_output_aliases=aliases,
        scratch_shapes=[pltpu.VMEM((nb, HGRN_HEADS, HGRN_DK, HGRN_DV), F32),
                        pltpu.VMEM((rows, d), BF16),
                        pltpu.VMEM((rows // hc, d // LANES, hc, LANES), F32),
                        pltpu.VMEM((rows // hc, hc, d), F32)],
        compiler_params=pltpu.CompilerParams(
            dimension_semantics=("arbitrary", "arbitrary"),
            vmem_limit_bytes=V7X_VMEM_LIMIT_BYTES),
        name=f"trunk_layer{layer}_{'sample' if emit_v else 'prompt'}",
    )(*args)


def _split_bf16(a):
    hi = a.astype(BF16)
    return hi, (a - hi.astype(F32)).astype(BF16)


def _ada_body(c_ref, w_ref, b_ref, o_ref):
    n = c_ref.shape[0]
    s_hi, s_lo = _split_bf16(_silu(c_ref[...]))
    w_hi, w_lo = _split_bf16(w_ref[0])
    both = jnp.dot(jnp.concatenate([s_hi, s_lo], axis=0), w_hi, preferred_element_type=F32)
    o_ref[0] = both[:n] + both[n:] + jnp.dot(s_hi, w_lo, preferred_element_type=F32) + b_ref[0]


def _ada_modulation(c_all, w_ada, b_ada):
    depth, d, d3 = w_ada.shape
    n = c_all.shape[0]
    return pl.pallas_call(
        _ada_body,
        grid=(depth, d3 // d),
        in_specs=[pl.BlockSpec((n, d), lambda l, j: (0, 0)),
                  pl.BlockSpec((1, d, d), lambda l, j: (l, 0, j)),
                  pl.BlockSpec((1, 1, d), lambda l, j: (l, 0, j))],
        out_specs=pl.BlockSpec((1, n, d), lambda l, j: (l, 0, j)),
        out_shape=jax.ShapeDtypeStruct((depth, n, d3), F32),
        compiler_params=pltpu.CompilerParams(dimension_semantics=("arbitrary", "arbitrary")),
        name="ada_modulation",
    )(c_all, w_ada, b_ada.reshape(depth, 1, d3))


def kernel(x_prompt, x_sample, state_hgrn, c_prompt, c_sample, w_ada, b_ada, norm_g, w_in, ln_v_g, ln_v_b,
           w_s, b_s, lb_raw, gnorm_g, w_pa, w_pb, w_o, final_g):
    depth, d = norm_g.shape
    n_prompt = x_prompt.shape[0]
    mod = _ada_modulation(jnp.concatenate([c_prompt, c_sample], axis=0), w_ada, b_ada)
    mod = mod.reshape(depth, -1, 3, d)

    xp, xs = x_prompt, x_sample
    n_sample = x_sample.shape[0]
    sp = jnp.zeros((depth, n_prompt, HGRN_HEADS, HGRN_DK, HGRN_DV), F32)
    ss = jnp.zeros((depth, n_sample, HGRN_HEADS, HGRN_DK, HGRN_DV), F32)
    vs = jnp.zeros((depth,) + x_sample.shape, F32)
    for l in range(depth):
        (w_in_l,) = _pack_rows_bf16([w_in], l)
        w_pa_l, w_pb_l, w_o_l = _pack_rows_bf16([w_pa, w_pb, w_o], l)
        w = dict(norm_g=norm_g[l][None], w_in=w_in_l, ln_v_g=ln_v_g[l][None],
                 ln_v_b=ln_v_b[l][None], w_s=w_s[l], b_s=b_s[l], lb_raw=lb_raw, gnorm_g=gnorm_g[l][None],
                 w_pa=w_pa_l, w_pb=w_pb_l, w_o=w_o_l, final_g=final_g[None])
        final = l == depth - 1
        xp, sp = _trunk_layer(l, xp, mod[l, :n_prompt], None, w, final, sp, None)
        xs, ss, vs = _trunk_layer(l, xs, mod[l, n_prompt:], state_hgrn, w, final, ss, vs)
    return (xp, xs, sp, ss, vs)
```

```python
import functools
import math
from typing import NamedTuple

import jax
import jax.numpy as jnp
from jax import lax
from jax.experimental import pallas as pl
from jax.experimental.pallas import tpu as pltpu

F32 = jnp.float32
BF16 = jnp.bfloat16

EPS = 1e-6
LANES = 128
GMLP_CHUNK = 128
GMLP_GROUPS = 4
HGRN_HEADS = 8
HGRN_DK = 128
HGRN_DV = 128
N_IN_SLABS = 9
SLAB_U, SLAB_V, SLAB_ZA, SLAB_Q, SLAB_F, SLAB_I, SLAB_ZB, SLAB_GA, SLAB_GB = range(N_IN_SLABS)

HGRN_CHUNK = 128
HGRN_SUBBLOCK = 32
SUBLANES = 8
EXP2_CLAMP = 115.0
LOG2E = math.log2(math.e)
TILE_ROWS = 256
PACK_BLOCK_COLS = 1024
V7X_VMEM_LIMIT_BYTES = 60000 * 1024

_NT = (((1,), (1,)), ((), ()))


class _Cfg(NamedTuple):
    layer: int
    nb: int
    tt: int
    gc: int
    hc: int
    has_s0: bool
    emit_v: bool
    final: bool


def _gelu_tanh(x):
    c = math.sqrt(2.0 / math.pi)
    half_x = 0.5 * x
    return half_x + half_x * jnp.tanh(x * (c + (c * 0.044715) * (x * x)))


def _exp_neg(x):
    return jnp.exp2(x * (-LOG2E))


def _sigmoid(x):
    return 1.0 / (1.0 + _exp_neg(x))


def _silu(x):
    return x * _sigmoid(x)


def _rms(x):
    return lax.rsqrt(jnp.mean(x * x, axis=-1, keepdims=True) + EPS)


def _pack_body(*refs):
    n = len(refs) // 2
    for w_ref, o_ref in zip(refs[:n], refs[n:]):
        o_ref[...] = pltpu.bitcast(w_ref[...].astype(BF16), jnp.uint32)


def _pack_rows_bf16(w_stacks, layer):
    _, k, n = w_stacks[0].shape
    assert all(w.shape[1:] == (k, n) for w in w_stacks)
    bn = min(n, PACK_BLOCK_COLS)
    assert n % bn == 0
    return pl.pallas_call(
        _pack_body,
        grid=(n // bn,),
        in_specs=[pl.BlockSpec((None, k, bn), lambda j: (layer, 0, j))] * len(w_stacks),
        out_specs=[pl.BlockSpec((k // 2, bn), lambda j: (0, j))] * len(w_stacks),
        out_shape=[jax.ShapeDtypeStruct((k // 2, n), jnp.uint32)] * len(w_stacks),
        compiler_params=pltpu.CompilerParams(dimension_semantics=("arbitrary",)),
        name="pack_bf16_rows",
    )(*w_stacks)


def _as_bf16(packed):
    return pltpu.bitcast(packed, BF16)


def _row_bcast(a, idx, n_rep):
    return jnp.broadcast_to(a[idx:idx + 1, :], (n_rep, a.shape[1]))


def _forget_lower_bound(lbraw_ref, layer):
    raw = lbraw_ref[...]
    mx = jnp.max(raw, axis=0, keepdims=True)
    ex = jnp.exp(raw - mx)
    den = jnp.sum(ex, axis=0, keepdims=True)
    num = jnp.zeros_like(den)
    for j in range(1, layer + 1):
        num = num + ex[j:j + 1, :]
    return num / den


def _hgrn_prepare(qs, kk, lf2, loc_ref):
    c, d = qs.shape
    dg = HGRN_SUBBLOCK
    nblk = c // dg
    rep = SUBLANES // nblk
    kidx = lax.broadcasted_iota(jnp.int32, (SUBLANES, d), 0)
    slabs = []
    for j in range(c // SUBLANES):
        v = lf2[j * SUBLANES:(j + 1) * SUBLANES]
        sh = 1
        while sh < SUBLANES:
            v = v + jnp.where(kidx >= sh, pltpu.roll(v, sh, 0), 0.0)
            sh *= 2
        if j % (dg // SUBLANES):
            v = v + _row_bcast(slabs[-1], SUBLANES - 1, SUBLANES)
        slabs.append(v)
    loc = jnp.concatenate(slabs, axis=0)
    n_lane_tiles = d // LANES
    for t in range(n_lane_tiles):
        loc_ref[t] = loc[:, t * LANES:(t + 1) * LANES]
    step = dg // rep
    part = jnp.concatenate([loc_ref[t, pl.ds(step - 1, SUBLANES, stride=step), :]
                            for t in range(n_lane_tiles)], axis=1)
    tot = part
    sh = 1
    while sh < rep:
        tot = jnp.where((kidx & (rep - 1)) >= rep - sh, tot, pltpu.roll(tot, SUBLANES - sh, 0))
        sh *= 2
    r_end = tot
    sh = rep
    while sh < SUBLANES:
        r_end = r_end + jnp.where(kidx >= sh, pltpu.roll(r_end, sh, 0), 0.0)
        sh *= 2
    r_start = r_end - tot
    b_last = r_end[SUBLANES - 1:SUBLANES, :]

    tot_rows = jnp.concatenate([_row_bcast(tot, j * rep, dg) for j in range(nblk)], axis=0)
    q_hat = qs * jnp.exp2(loc)
    k_hat = kk * jnp.exp2(tot_rows - loc)
    q_dg = q_hat.astype(BF16)
    k_dg = (kk * jnp.exp2(jnp.minimum(-loc, EXP2_CLAMP))).astype(BF16)

    def scaled_f32(fac, use_q, unit):
        parts = []
        for j in range(nblk):
            rows_j = (q_hat if use_q[j] else k_hat)[j * dg:(j + 1) * dg]
            parts.append(rows_j if unit[j] else rows_j * _row_bcast(fac, j * rep, dg))
        return jnp.concatenate(parts, axis=0)

    def scaled(fac, use_q, unit):
        return scaled_f32(fac, use_q, unit).astype(BF16)

    q_in = scaled(jnp.exp2(r_start), [True] * nblk, [j == 0 for j in range(nblk)])
    k_end_t = scaled_f32(jnp.exp2(b_last - r_end), [False] * nblk,
                         [j == nblk - 1 for j in range(nblk)]).T.astype(BF16)
    decay_t = jnp.broadcast_to(jnp.exp2(b_last), (HGRN_DV, d)).T

    ti = lax.broadcasted_iota(jnp.int32, (c, c), 0)
    si = lax.broadcasted_iota(jnp.int32, (c, c), 1)
    levels = []
    hb = nblk // 2
    while hb >= 1:
        m = None
        for g in reversed(range(nblk // (2 * hb))):
            row = _row_bcast(r_start, (g * 2 * hb + hb) * rep, SUBLANES)
            m = row if m is None else jnp.where(kidx < (g + 1) * 2 * hb * rep, row, m)
        fac = jnp.exp2(jnp.where((kidx & (hb * rep)) != 0, r_start - m, m - r_end))
        unit = [(j % hb == 0) if (j & hb) else (j % hb == hb - 1) for j in range(nblk)]
        x = scaled(fac, [(j & hb) != 0 for j in range(nblk)], unit)
        half = hb * dg
        mask = ((ti ^ si) < 2 * half) & ((ti & half) != 0) & ((si & half) == 0)
        levels.append((x, mask))
        hb //= 2
    worst = jnp.max(jnp.max(-tot, axis=1, keepdims=True), axis=0, keepdims=True)
    mask_dg = ((ti ^ si) < dg) & (si <= ti) & (worst <= EXP2_CLAMP)
    return dict(q_in=q_in, k_end_t=k_end_t, decay_t=decay_t, levels=levels, q_dg=q_dg, k_dg=k_dg,
                mask_dg=mask_dg, flagged=worst[0, 0] > EXP2_CLAMP, loc=loc, qs=qs, kk=kk)


def _hgrn_intra(p):
    out = []
    for h in range(HGRN_HEADS):
        hs = slice(h * HGRN_DK, (h + 1) * HGRN_DK)
        a = jnp.where(p["mask_dg"],
                      lax.dot_general(p["q_dg"][:, hs], p["k_dg"][:, hs], _NT, preferred_element_type=F32), 0.0)
        for x, mask in p["levels"]:
            xh = x[:, hs]
            a = jnp.where(mask, lax.dot_general(xh, xh, _NT, preferred_element_type=F32), a)
        out.append(a.astype(BF16))
    return out


def _hgrn_apply(p, a_heads, iv, st_ref, n, o_ref):
    c = iv.shape[0]
    for h in range(HGRN_HEADS):
        hs = slice(h * HGRN_DK, (h + 1) * HGRN_DK)
        vs = slice(h * HGRN_DV, (h + 1) * HGRN_DV)
        st = st_ref[n, h]
        stacked = jnp.dot(jnp.concatenate([a_heads[h], p["k_end_t"][hs, :]], axis=0), iv[:, vs],
                          preferred_element_type=F32)
        o_ref[:, vs] = jnp.dot(p["q_in"][:, hs], st.astype(BF16), preferred_element_type=F32) + stacked[:c]
        st_ref[n, h] = st * p["decay_t"][hs, :] + stacked[c:]


def _hgrn_exact_diag(p, iv):
    qs, kk, loc = p["qs"], p["kk"], p["loc"]
    c, d = qs.shape
    dg = HGRN_SUBBLOCK
    ivf = iv.astype(F32)
    inblk = lax.broadcasted_iota(jnp.int32, (c, d), 0) & (dg - 1)

    def body(dd, acc):
        valid = inblk >= dd
        expo = jnp.where(valid, loc - pltpu.roll(loc, dd, 0), 0.0)
        w = jnp.where(valid, qs * pltpu.roll(kk, dd, 0) * jnp.exp2(expo), 0.0)
        ir = pltpu.roll(ivf, dd, 0)
        parts = []
        for h in range(HGRN_HEADS):
            hs = slice(h * HGRN_DK, (h + 1) * HGRN_DK)
            vs = slice(h * HGRN_DV, (h + 1) * HGRN_DV)
            parts.append(jnp.sum(w[:, hs], axis=-1, keepdims=True) * ir[:, vs])
        return acc + jnp.concatenate(parts, axis=1)

    return lax.fori_loop(0, dg, body, jnp.zeros((c, d), F32))


def _layer_body(cfg, *refs):
    refs = list(refs)
    x_ref, mod_ref = refs[:2]
    pos = 2
    s0_ref = None
    if cfg.has_s0:
        s0_ref = refs[pos]
        pos += 1
    (normg_ref, win_ref, lnvg_ref, lnvb_ref, ws_ref, bs_ref, lbraw_ref, gng_ref,
     wpa_ref, wpb_ref, wo_ref, fing_ref) = refs[pos:pos + 12]
    pos += 12
    pos += 2 if cfg.emit_v else 1
    y_ref, sout_ref = refs[pos:pos + 2]
    pos += 2
    v_ref = None
    if cfg.emit_v:
        v_ref = refs[pos]
        pos += 1
    st_ref, yb_ref, loc_ref, o_ref = refs[pos:pos + 4]

    nb, tt = cfg.nb, cfg.tt
    rows = nb * tt
    d = x_ref.shape[-1]
    t_idx = pl.program_id(1)
    n_t = pl.num_programs(1)

    @pl.when(t_idx == 0)
    def _():
        for n in range(nb):
            for hd in range(HGRN_HEADS):
                if cfg.has_s0:
                    st_ref[n, hd] = s0_ref[n, hd]
                else:
                    st_ref[n, hd] = jnp.zeros((HGRN_DK, HGRN_DV), F32)

    hs = []
    for n in range(nb):
        x = x_ref[n]
        shift = mod_ref[n, 0:1, :]
        scale = mod_ref[n, 1:2, :]
        hs.append(((x * _rms(x)) * (normg_ref[...] * (1.0 + scale)) + shift).astype(BF16))
    h = hs[0] if nb == 1 else jnp.concatenate(hs, axis=0)

    def proj(slab):
        return jnp.dot(h, _as_bf16(win_ref[:, slab * d:(slab + 1) * d]), preferred_element_type=F32)

    z = proj(SLAB_F)
    pq = proj(SLAB_Q)
    iv = proj(SLAB_I).astype(BF16)
    pv = proj(SLAB_V)
    pu = proj(SLAB_U)
    pza = proj(SLAB_ZA)

    t_neg = jnp.abs(z) * (-LOG2E)
    e = jnp.exp2(t_neg)
    inv = 1.0 / (1.0 + e)
    sig_neg = jnp.where(z >= 0.0, e, 1.0) * inv
    if cfg.layer == 0:
        lf2 = jnp.where(z >= 0.0, 0.0, t_neg) + jnp.log2(inv)
        kk = sig_neg
    else:
        lb = _forget_lower_bound(lbraw_ref, cfg.layer)
        f = lb + (1.0 - lb) * (jnp.where(z >= 0.0, 1.0, e) * inv)
        lf2 = jnp.where(f > 0.0, jnp.log2(f), t_neg)
        kk = (1.0 - lb) * sig_neg
    qs = _silu(pq)

    v = _gelu_tanh(pv)
    mu = jnp.mean(v, axis=-1, keepdims=True)
    vc = v - mu
    var = jnp.mean(vc * vc, axis=-1, keepdims=True)
    vn = vc * lax.rsqrt(var + EPS) * lnvg_ref[...] + lnvb_ref[...]
    if cfg.emit_v:
        for n in range(nb):
            v_ref[n] = vn[n * tt:(n + 1) * tt]
    vb = vn.astype(BF16)
    gc = cfg.gc
    gdim = d // GMLP_GROUPS
    tril = (lax.broadcasted_iota(jnp.int32, (gc, gc), 0) >= lax.broadcasted_iota(jnp.int32, (gc, gc), 1))
    w_tril = [jnp.where(tril, ws_ref[g], 0.0).astype(BF16) for g in range(GMLP_GROUPS)]
    s_chunks = []
    for j in range(rows // gc):
        parts = [jnp.dot(w_tril[g], vb[j * gc:(j + 1) * gc, g * gdim:(g + 1) * gdim],
                         preferred_element_type=F32) + bs_ref[:, g:g + 1]
                 for g in range(GMLP_GROUPS)]
        s_chunks.append(jnp.concatenate(parts, axis=1))
    s = s_chunks[0] if len(s_chunks) == 1 else jnp.concatenate(s_chunks, axis=0)

    pzb = proj(SLAB_ZB)
    pga = proj(SLAB_GA)

    hc = cfg.hc
    cps = tt // hc
    n_chunks = rows // hc
    prep = [_hgrn_prepare(qs[j * hc:(j + 1) * hc], kk[j * hc:(j + 1) * hc], lf2[j * hc:(j + 1) * hc],
                          loc_ref.at[j])
            for j in range(n_chunks)]
    a_heads = [_hgrn_intra(p) for p in prep]

    ya = (_gelu_tanh(pu) * s * _silu(pza)).astype(BF16)
    m_a = _sigmoid(pga) * jnp.dot(ya, _as_bf16(wpa_ref[...]), preferred_element_type=F32)

    gate_b = _silu(pzb)
    gng = gng_ref[...]
    for j in range(n_chunks):
        _hgrn_apply(prep[j], a_heads[j], iv[j * hc:(j + 1) * hc], st_ref, j // cps, o_ref.at[j])
    sig_gb = _sigmoid(proj(SLAB_GB))

    def finish():
        for j in range(n_chunks):
            for hd in range(HGRN_HEADS):
                vs = slice(hd * HGRN_DV, (hd + 1) * HGRN_DV)
                o = o_ref[j, :, vs]
                yb_ref[j * hc:(j + 1) * hc, vs] = (o * _rms(o) * gng[:, vs]
                                                   * gate_b[j * hc:(j + 1) * hc, vs]).astype(BF16)
        m_b = sig_gb * jnp.dot(yb_ref[...], _as_bf16(wpb_ref[...]), preferred_element_type=F32)
        upd = jnp.dot((m_a + m_b).astype(BF16), _as_bf16(wo_ref[...]), preferred_element_type=F32)
        for n in range(nb):
            gate = mod_ref[n, 2:3, :]
            xo = x_ref[n] + gate * upd[n * tt:(n + 1) * tt]
            if cfg.final:
                xo = (xo * _rms(xo)) * fing_ref[...]
            y_ref[n] = xo

    finish()

    any_flagged = prep[0]["flagged"]
    for p in prep[1:]:
        any_flagged = jnp.logical_or(any_flagged, p["flagged"])

    @pl.when(any_flagged)
    def _():
        for j in range(n_chunks):
            @pl.when(prep[j]["flagged"])
            def _():
                o_ref[j] = o_ref[j] + _hgrn_exact_diag(prep[j], iv[j * hc:(j + 1) * hc])
        finish()

    @pl.when(t_idx == n_t - 1)
    def _():
        for n in range(nb):
            for hd in range(HGRN_HEADS):
                sout_ref[n, hd] = st_ref[n, hd]


def _tiling(batch, seq):
    chunk = min(seq, GMLP_CHUNK)
    assert seq % chunk == 0 and chunk % HGRN_SUBBLOCK == 0 and SUBLANES % (chunk // HGRN_SUBBLOCK) == 0
    if seq >= TILE_ROWS and batch % (TILE_ROWS // chunk):
        assert seq % TILE_ROWS == 0
        return 1, TILE_ROWS, chunk
    seq = min(seq, chunk)
    nb = max(1, TILE_ROWS // seq)
    while batch % nb:
        nb -= 1
    return nb, seq, chunk


def _trunk_layer(layer, x, mod, s0, w, final, state_stack, v_stack):
    emit_v = v_stack is not None
    batch, seq, d = x.shape
    nb, tt, chunk = _tiling(batch, seq)
    hc = min(chunk, HGRN_CHUNK)
    n_b, n_t = batch // nb, seq // tt
    cfg = _Cfg(layer=layer, nb=nb, tt=tt, gc=chunk, hc=hc,
               has_s0=s0 is not None, emit_v=emit_v, final=final)
    rows = nb * tt

    def full(a):
        nd = a.ndim
        return pl.BlockSpec(a.shape, lambda b, t, _nd=nd: (0,) * _nd)

    tok_spec = pl.BlockSpec((nb, tt, d), lambda b, t: (b, t, 0))
    st_spec = pl.BlockSpec((None, nb, HGRN_HEADS, HGRN_DK, HGRN_DV), lambda b, t: (layer, b, 0, 0, 0))
    v_spec = pl.BlockSpec((None, nb, tt, d), lambda b, t: (layer, b, t, 0))
    ws = w["w_s"][:, :chunk, :chunk]
    bs_t = w["b_s"][:, :chunk].T
    params = [w["norm_g"], w["w_in"], w["ln_v_g"], w["ln_v_b"], ws, bs_t, w["lb_raw"], w["gnorm_g"],
              w["w_pa"], w["w_pb"], w["w_o"], w["final_g"]]

    args = [x, mod]
    in_specs = [tok_spec, pl.BlockSpec((nb, 3, d), lambda b, t: (b, 0, 0))]
    if s0 is not None:
        args.append(s0)
        in_specs.append(st_spec)
    args += params
    in_specs += [full(a) for a in params]
    stacks = [state_stack] + ([v_stack] if emit_v else [])
    aliases = {len(args) + i: 1 + i for i in range(len(stacks))}
    args += stacks
    in_specs += [pl.BlockSpec(memory_space=pl.ANY)] * len(stacks)
    out_shape = [jax.ShapeDtypeStruct(x.shape, F32)] + [jax.ShapeDtypeStruct(a.shape, F32) for a in stacks]
    out_specs = [tok_spec, st_spec] + ([v_spec] if emit_v else [])
    return pl.pallas_call(
        functools.partial(_layer_body, cfg),
        grid=(n_b, n_t),
        in_specs=in_specs,
        out_specs=out_specs,
        out_shape=out_shape,
        input_output_aliases=aliases,
        scratch_shapes=[pltpu.VMEM((nb, HGRN_HEADS, HGRN_DK, HGRN_DV), F32),
                        pltpu.VMEM((rows, d), BF16),
                        pltpu.VMEM((rows // hc, d // LANES, hc, LANES), F32),
                        pltpu.VMEM((rows // hc, hc, d), F32)],
        compiler_params=pltpu.CompilerParams(
            dimension_semantics=("arbitrary", "arbitrary"),
            vmem_limit_bytes=V7X_VMEM_LIMIT_BYTES),
        name=f"trunk_layer{layer}_{'sample' if emit_v else 'prompt'}",
    )(*args)


def _split_bf16(a):
    hi = a.astype(BF16)
    return hi, (a - hi.astype(F32)).astype(BF16)


def _ada_body(c_ref, w_ref, b_ref, o_ref):
    n = c_ref.shape[0]
    s_hi, s_lo = _split_bf16(_silu(c_ref[...]))
    w_hi, w_lo = _split_bf16(w_ref[0])
    both = jnp.dot(jnp.concatenate([s_hi, s_lo], axis=0), w_hi, preferred_element_type=F32)
    o_ref[0] = both[:n] + both[n:] + jnp.dot(s_hi, w_lo, preferred_element_type=F32) + b_ref[0]


def _ada_modulation(c_all, w_ada, b_ada):
    depth, d, d3 = w_ada.shape
    n = c_all.shape[0]
    return pl.pallas_call(
        _ada_body,
        grid=(depth, d3 // d),
        in_specs=[pl.BlockSpec((n, d), lambda l, j: (0, 0)),
                  pl.BlockSpec((1, d, d), lambda l, j: (l, 0, j)),
                  pl.BlockSpec((1, 1, d), lambda l, j: (l, 0, j))],
        out_specs=pl.BlockSpec((1, n, d), lambda l, j: (l, 0, j)),
        out_shape=jax.ShapeDtypeStruct((depth, n, d3), F32),
        compiler_params=pltpu.CompilerParams(dimension_semantics=("arbitrary", "arbitrary")),
        name="ada_modulation",
    )(c_all, w_ada, b_ada.reshape(depth, 1, d3))


def kernel(x_prompt, x_sample, state_hgrn, c_prompt, c_sample, w_ada, b_ada, norm_g, w_in, ln_v_g, ln_v_b,
           w_s, b_s, lb_raw, gnorm_g, w_pa, w_pb, w_o, final_g):
    depth, d = norm_g.shape
    n_prompt = x_prompt.shape[0]
    mod = _ada_modulation(jnp.concatenate([c_prompt, c_sample], axis=0), w_ada, b_ada)
    mod = mod.reshape(depth, -1, 3, d)

    xp, xs = x_prompt, x_sample
    n_sample = x_sample.shape[0]
    sp = jnp.zeros((depth, n_prompt, HGRN_HEADS, HGRN_DK, HGRN_DV), F32)
    ss = jnp.zeros((depth, n_sample, HGRN_HEADS, HGRN_DK, HGRN_DV), F32)
    vs = jnp.zeros((depth,) + x_sample.shape, F32)
    for l in range(depth):
        (w_in_l,) = _pack_rows_bf16([w_in], l)
        w_pa_l, w_pb_l, w_o_l = _pack_rows_bf16([w_pa, w_pb, w_o], l)
        w = dict(norm_g=norm_g[l][None], w_in=w_in_l, ln_v_g=ln_v_g[l][None],
                 ln_v_b=ln_v_b[l][None], w_s=w_s[l], b_s=b_s[l], lb_raw=lb_raw, gnorm_g=gnorm_g[l][None],
                 w_pa=w_pa_l, w_pb=w_pb_l, w_o=w_o_l, final_g=final_g[None])
        final = l == depth - 1
        xp, sp = _trunk_layer(l, xp, mod[l, :n_prompt], None, w, final, sp, None)
        xs, ss, vs = _trunk_layer(l, xs, mod[l, n_prompt:], state_hgrn, w, final, ss, vs)
    return (xp, xs, sp, ss, vs)
```

```python
import functools
import math
from typing import NamedTuple

import jax
import jax.numpy as jnp
from jax import lax
from jax.experimental import pallas as pl
from jax.experimental.pallas import tpu as pltpu

F32 = jnp.float32
BF16 = jnp.bfloat16

EPS = 1e-6
LANES = 128
GMLP_CHUNK = 128
GMLP_GROUPS = 4
HGRN_HEADS = 8
HGRN_DK = 128
HGRN_DV = 128
N_IN_SLABS = 9
SLAB_U, SLAB_V, SLAB_ZA, SLAB_Q, SLAB_F, SLAB_I, SLAB_ZB, SLAB_GA, SLAB_GB = range(N_IN_SLABS)

HGRN_CHUNK = 128
HGRN_SUBBLOCK = 32
SUBLANES = 8
EXP2_CLAMP = 115.0
LOG2E = math.log2(math.e)
TILE_ROWS = 256
PACK_BLOCK_COLS = 1024
V7X_VMEM_LIMIT_BYTES = 60000 * 1024

_NT = (((1,), (1,)), ((), ()))


class _Cfg(NamedTuple):
    layer: int
    nb: int
    tt: int
    gc: int
    hc: int
    has_s0: bool
    emit_v: bool
    final: bool


def _gelu_tanh(x):
    c = math.sqrt(2.0 / math.pi)
    half_x = 0.5 * x
    return half_x + half_x * jnp.tanh(x * (c + (c * 0.044715) * (x * x)))


def _exp_neg(x):
    return jnp.exp2(x * (-LOG2E))


def _sigmoid(x):
    return 1.0 / (1.0 + _exp_neg(x))


def _silu(x):
    return x * _sigmoid(x)


def _rms(x):
    return lax.rsqrt(jnp.mean(x * x, axis=-1, keepdims=True) + EPS)


def _pack_body(*refs):
    n = len(refs) // 2
    for w_ref, o_ref in zip(refs[:n], refs[n:]):
        o_ref[...] = pltpu.bitcast(w_ref[...].astype(BF16), jnp.uint32)


def _pack_rows_bf16(w_stacks, layer):
    _, k, n = w_stacks[0].shape
    assert all(w.shape[1:] == (k, n) for w in w_stacks)
    bn = min(n, PACK_BLOCK_COLS)
    assert n % bn == 0
    return pl.pallas_call(
        _pack_body,
        grid=(n // bn,),
        in_specs=[pl.BlockSpec((None, k, bn), lambda j: (layer, 0, j))] * len(w_stacks),
        out_specs=[pl.BlockSpec((k // 2, bn), lambda j: (0, j))] * len(w_stacks),
        out_shape=[jax.ShapeDtypeStruct((k // 2, n), jnp.uint32)] * len(w_stacks),
        compiler_params=pltpu.CompilerParams(dimension_semantics=("arbitrary",)),
        name="pack_bf16_rows",
    )(*w_stacks)


def _as_bf16(packed):
    return pltpu.bitcast(packed, BF16)


def _row_bcast(a, idx, n_rep):
    return jnp.broadcast_to(a[idx:idx + 1, :], (n_rep, a.shape[1]))


def _forget_lower_bound(lbraw_ref, layer):
    raw = lbraw_ref[...]
    mx = jnp.max(raw, axis=0, keepdims=True)
    ex = jnp.exp(raw - mx)
    den = jnp.sum(ex, axis=0, keepdims=True)
    num = jnp.zeros_like(den)
    for j in range(1, layer + 1):
        num = num + ex[j:j + 1, :]
    return num / den


def _hgrn_prepare(qs, kk, lf2, loc_ref):
    c, d = qs.shape
    dg = HGRN_SUBBLOCK
    nblk = c // dg
    rep = SUBLANES // nblk
    kidx = lax.broadcasted_iota(jnp.int32, (SUBLANES, d), 0)
    slabs = []
    for j in range(c // SUBLANES):
        v = lf2[j * SUBLANES:(j + 1) * SUBLANES]
        sh = 1
        while sh < SUBLANES:
            v = v + jnp.where(kidx >= sh, pltpu.roll(v, sh, 0), 0.0)
            sh *= 2
        if j % (dg // SUBLANES):
            v = v + _row_bcast(slabs[-1], SUBLANES - 1, SUBLANES)
        slabs.append(v)
    loc = jnp.concatenate(slabs, axis=0)
    n_lane_tiles = d // LANES
    for t in range(n_lane_tiles):
        loc_ref[t] = loc[:, t * LANES:(t + 1) * LANES]
    step = dg // rep
    part = jnp.concatenate([loc_ref[t, pl.ds(step - 1, SUBLANES, stride=step), :]
                            for t in range(n_lane_tiles)], axis=1)
    tot = part
    sh = 1
    while sh < rep:
        tot = jnp.where((kidx & (rep - 1)) >= rep - sh, tot, pltpu.roll(tot, SUBLANES - sh, 0))
        sh *= 2
    r_end = tot
    sh = rep
    while sh < SUBLANES:
        r_end = r_end + jnp.where(kidx >= sh, pltpu.roll(r_end, sh, 0), 0.0)
        sh *= 2
    r_start = r_end - tot
    b_last = r_end[SUBLANES - 1:SUBLANES, :]

    tot_rows = jnp.concatenate([_row_bcast(tot, j * rep, dg) for j in range(nblk)], axis=0)
    q_hat = qs * jnp.exp2(loc)
    k_hat = kk * jnp.exp2(tot_rows - loc)
    q_dg = q_hat.astype(BF16)
    k_dg = (kk * jnp.exp2(jnp.minimum(-loc, EXP2_CLAMP))).astype(BF16)

    def scaled_f32(fac, use_q, unit):
        parts = []
        for j in range(nblk):
            rows_j = (q_hat if use_q[j] else k_hat)[j * dg:(j + 1) * dg]
            parts.append(rows_j if unit[j] else rows_j * _row_bcast(fac, j * rep, dg))
        return jnp.concatenate(parts, axis=0)

    def scaled(fac, use_q, unit):
        return scaled_f32(fac, use_q, unit).astype(BF16)

    q_in = scaled(jnp.exp2(r_start), [True] * nblk, [j == 0 for j in range(nblk)])
    k_end_t = scaled_f32(jnp.exp2(b_last - r_end), [False] * nblk,
                         [j == nblk - 1 for j in range(nblk)]).T.astype(BF16)
    decay_t = jnp.broadcast_to(jnp.exp2(b_last), (HGRN_DV, d)).T

    ti = lax.broadcasted_iota(jnp.int32, (c, c), 0)
    si = lax.broadcasted_iota(jnp.int32, (c, c), 1)
    levels = []
    hb = nblk // 2
    while hb >= 1:
        m = None
        for g in reversed(range(nblk // (2 * hb))):
            row = _row_bcast(r_start, (g * 2 * hb + hb) * rep, SUBLANES)
            m = row if m is None else jnp.where(kidx < (g + 1) * 2 * hb * rep, row, m)
        fac = jnp.exp2(jnp.where((kidx & (hb * rep)) != 0, r_start - m, m - r_end))
        unit = [(j % hb == 0) if (j & hb) else (j % hb == hb - 1) for j in range(nblk)]
        x = scaled(fac, [(j & hb) != 0 for j in range(nblk)], unit)
        half = hb * dg
        mask = ((ti ^ si) < 2 * half) & ((ti & half) != 0) & ((si & half) == 0)
        levels.append((x, mask))
        hb //= 2
    worst = jnp.max(jnp.max(-tot, axis=1, keepdims=True), axis=0, keepdims=True)
    mask_dg = ((ti ^ si) < dg) & (si <= ti) & (worst <= EXP2_CLAMP)
    return dict(q_in=q_in, k_end_t=k_end_t, decay_t=decay_t, levels=levels, q_dg=q_dg, k_dg=k_dg,
                mask_dg=mask_dg, flagged=worst[0, 0] > EXP2_CLAMP, loc=loc, qs=qs, kk=kk)


def _hgrn_intra(p):
    out = []
    for h in range(HGRN_HEADS):
        hs = slice(h * HGRN_DK, (h + 1) * HGRN_DK)
        a = jnp.where(p["mask_dg"],
                      lax.dot_general(p["q_dg"][:, hs], p["k_dg"][:, hs], _NT, preferred_element_type=F32), 0.0)
        for x, mask in p["levels"]:
            xh = x[:, hs]
            a = jnp.where(mask, lax.dot_general(xh, xh, _NT, preferred_element_type=F32), a)
        out.append(a.astype(BF16))
    return out


def _hgrn_apply(p, a_heads, iv, st_ref, n, o_ref):
    c = iv.shape[0]
    for h in range(HGRN_HEADS):
        hs = slice(h * HGRN_DK, (h + 1) * HGRN_DK)
        vs = slice(h * HGRN_DV, (h + 1) * HGRN_DV)
        st = st_ref[n, h]
        stacked = jnp.dot(jnp.concatenate([a_heads[h], p["k_end_t"][hs, :]], axis=0), iv[:, vs],
                          preferred_element_type=F32)
        o_ref[:, vs] = jnp.dot(p["q_in"][:, hs], st.astype(BF16), preferred_element_type=F32) + stacked[:c]
        st_ref[n, h] = st * p["decay_t"][hs, :] + stacked[c:]


def _hgrn_exact_diag(p, iv):
    qs, kk, loc = p["qs"], p["kk"], p["loc"]
    c, d = qs.shape
    dg = HGRN_SUBBLOCK
    ivf = iv.astype(F32)
    inblk = lax.broadcasted_iota(jnp.int32, (c, d), 0) & (dg - 1)

    def body(dd, acc):
        valid = inblk >= dd
        expo = jnp.where(valid, loc - pltpu.roll(loc, dd, 0), 0.0)
        w = jnp.where(valid, qs * pltpu.roll(kk, dd, 0) * jnp.exp2(expo), 0.0)
        ir = pltpu.roll(ivf, dd, 0)
        parts = []
        for h in range(HGRN_HEADS):
            hs = slice(h * HGRN_DK, (h + 1) * HGRN_DK)
            vs = slice(h * HGRN_DV, (h + 1) * HGRN_DV)
            parts.append(jnp.sum(w[:, hs], axis=-1, keepdims=True) * ir[:, vs])
        return acc + jnp.concatenate(parts, axis=1)

    return lax.fori_loop(0, dg, body, jnp.zeros((c, d), F32))


def _layer_body(cfg, *refs):
    refs = list(refs)
    x_ref, mod_ref = refs[:2]
    pos = 2
    s0_ref = None
    if cfg.has_s0:
        s0_ref = refs[pos]
        pos += 1
    (normg_ref, win_ref, lnvg_ref, lnvb_ref, ws_ref, bs_ref, lbraw_ref, gng_ref,
     wpa_ref, wpb_ref, wo_ref, fing_ref) = refs[pos:pos + 12]
    pos += 12
    pos += 2 if cfg.emit_v else 1
    y_ref, sout_ref = refs[pos:pos + 2]
    pos += 2
    v_ref = None
    if cfg.emit_v:
        v_ref = refs[pos]
        pos += 1
    st_ref, yb_ref, loc_ref, o_ref = refs[pos:pos + 4]

    nb, tt = cfg.nb, cfg.tt
    rows = nb * tt
    d = x_ref.shape[-1]
    t_idx = pl.program_id(1)
    n_t = pl.num_programs(1)

    @pl.when(t_idx == 0)
    def _():
        for n in range(nb):
            for hd in range(HGRN_HEADS):
                if cfg.has_s0:
                    st_ref[n, hd] = s0_ref[n, hd]
                else:
                    st_ref[n, hd] = jnp.zeros((HGRN_DK, HGRN_DV), F32)

    hs = []
    for n in range(nb):
        x = x_ref[n]
        shift = mod_ref[n, 0:1, :]
        scale = mod_ref[n, 1:2, :]
        hs.append(((x * _rms(x)) * (normg_ref[...] * (1.0 + scale)) + shift).astype(BF16))
    h = hs[0] if nb == 1 else jnp.concatenate(hs, axis=0)

    def proj(slab):
        return jnp.dot(h, _as_bf16(win_ref[:, slab * d:(slab + 1) * d]), preferred_element_type=F32)

    z = proj(SLAB_F)
    pq = proj(SLAB_Q)
    iv = proj(SLAB_I).astype(BF16)
    pv = proj(SLAB_V)
    pu = proj(SLAB_U)
    pza = proj(SLAB_ZA)

    t_neg = jnp.abs(z) * (-LOG2E)
    e = jnp.exp2(t_neg)
    inv = 1.0 / (1.0 + e)
    e_inv = e * inv
    sig_neg = jnp.where(z >= 0.0, e_inv, inv)
    if cfg.layer == 0:
        lf2 = jnp.where(z >= 0.0, 0.0, t_neg) + jnp.log2(inv)
        kk = sig_neg
    else:
        lb = _forget_lower_bound(lbraw_ref, cfg.layer)
        f = lb + (1.0 - lb) * jnp.where(z >= 0.0, inv, e_inv)
        lf2 = jnp.where(f > 0.0, jnp.log2(f), t_neg)
        kk = (1.0 - lb) * sig_neg
    qs = _silu(pq)

    v = _gelu_tanh(pv)
    mu = jnp.mean(v, axis=-1, keepdims=True)
    vc = v - mu
    var = jnp.mean(vc * vc, axis=-1, keepdims=True)
    vn = vc * lax.rsqrt(var + EPS) * lnvg_ref[...] + lnvb_ref[...]
    if cfg.emit_v:
        for n in range(nb):
            v_ref[n] = vn[n * tt:(n + 1) * tt]
    vb = vn.astype(BF16)
    gc = cfg.gc
    gdim = d // GMLP_GROUPS
    tril = (lax.broadcasted_iota(jnp.int32, (gc, gc), 0) >= lax.broadcasted_iota(jnp.int32, (gc, gc), 1))
    w_tril = [jnp.where(tril, ws_ref[g], 0.0).astype(BF16) for g in range(GMLP_GROUPS)]
    s_chunks = []
    for j in range(rows // gc):
        parts = [jnp.dot(w_tril[g], vb[j * gc:(j + 1) * gc, g * gdim:(g + 1) * gdim],
                         preferred_element_type=F32) + bs_ref[:, g:g + 1]
                 for g in range(GMLP_GROUPS)]
        s_chunks.append(jnp.concatenate(parts, axis=1))
    s = s_chunks[0] if len(s_chunks) == 1 else jnp.concatenate(s_chunks, axis=0)

    pzb = proj(SLAB_ZB)
    pga = proj(SLAB_GA)

    hc = cfg.hc
    cps = tt // hc
    n_chunks = rows // hc
    prep = [_hgrn_prepare(qs[j * hc:(j + 1) * hc], kk[j * hc:(j + 1) * hc], lf2[j * hc:(j + 1) * hc],
                          loc_ref.at[j])
            for j in range(n_chunks)]
    a_heads = [_hgrn_intra(p) for p in prep]

    ya = (_gelu_tanh(pu) * s * _silu(pza)).astype(BF16)
    m_a = _sigmoid(pga) * jnp.dot(ya, _as_bf16(wpa_ref[...]), preferred_element_type=F32)

    gate_b = _silu(pzb)
    gng = gng_ref[...]
    for j in range(n_chunks):
        _hgrn_apply(prep[j], a_heads[j], iv[j * hc:(j + 1) * hc], st_ref, j // cps, o_ref.at[j])
    sig_gb = _sigmoid(proj(SLAB_GB))

    def finish():
        for j in range(n_chunks):
            for hd in range(HGRN_HEADS):
                vs = slice(hd * HGRN_DV, (hd + 1) * HGRN_DV)
                o = o_ref[j, :, vs]
                yb_ref[j * hc:(j + 1) * hc, vs] = (o * _rms(o) * gng[:, vs]
                                                   * gate_b[j * hc:(j + 1) * hc, vs]).astype(BF16)
        m_b = sig_gb * jnp.dot(yb_ref[...], _as_bf16(wpb_ref[...]), preferred_element_type=F32)
        upd = jnp.dot((m_a + m_b).astype(BF16), _as_bf16(wo_ref[...]), preferred_element_type=F32)
        for n in range(nb):
            gate = mod_ref[n, 2:3, :]
            xo = x_ref[n] + gate * upd[n * tt:(n + 1) * tt]
            if cfg.final:
                xo = (xo * _rms(xo)) * fing_ref[...]
            y_ref[n] = xo

    finish()

    any_flagged = prep[0]["flagged"]
    for p in prep[1:]:
        any_flagged = jnp.logical_or(any_flagged, p["flagged"])

    @pl.when(any_flagged)
    def _():
        for j in range(n_chunks):
            @pl.when(prep[j]["flagged"])
            def _():
                o_ref[j] = o_ref[j] + _hgrn_exact_diag(prep[j], iv[j * hc:(j + 1) * hc])
        finish()

    @pl.when(t_idx == n_t - 1)
    def _():
        for n in range(nb):
            for hd in range(HGRN_HEADS):
                sout_ref[n, hd] = st_ref[n, hd]


def _tiling(batch, seq):
    chunk = min(seq, GMLP_CHUNK)
    assert seq % chunk == 0 and chunk % HGRN_SUBBLOCK == 0 and SUBLANES % (chunk // HGRN_SUBBLOCK) == 0
    if seq >= TILE_ROWS and batch % (TILE_ROWS // chunk):
        assert seq % TILE_ROWS == 0
        return 1, TILE_ROWS, chunk
    seq = min(seq, chunk)
    nb = max(1, TILE_ROWS // seq)
    while batch % nb:
        nb -= 1
    return nb, seq, chunk


def _trunk_layer(layer, x, mod, s0, w, final, state_stack, v_stack):
    emit_v = v_stack is not None
    batch, seq, d = x.shape
    nb, tt, chunk = _tiling(batch, seq)
    hc = min(chunk, HGRN_CHUNK)
    n_b, n_t = batch // nb, seq // tt
    cfg = _Cfg(layer=layer, nb=nb, tt=tt, gc=chunk, hc=hc,
               has_s0=s0 is not None, emit_v=emit_v, final=final)
    rows = nb * tt

    def full(a):
        nd = a.ndim
        return pl.BlockSpec(a.shape, lambda b, t, _nd=nd: (0,) * _nd)

    tok_spec = pl.BlockSpec((nb, tt, d), lambda b, t: (b, t, 0))
    st_spec = pl.BlockSpec((None, nb, HGRN_HEADS, HGRN_DK, HGRN_DV), lambda b, t: (layer, b, 0, 0, 0))
    v_spec = pl.BlockSpec((None, nb, tt, d), lambda b, t: (layer, b, t, 0))
    ws = w["w_s"][:, :chunk, :chunk]
    bs_t = w["b_s"][:, :chunk].T
    params = [w["norm_g"], w["w_in"], w["ln_v_g"], w["ln_v_b"], ws, bs_t, w["lb_raw"], w["gnorm_g"],
              w["w_pa"], w["w_pb"], w["w_o"], w["final_g"]]

    args = [x, mod]
    in_specs = [tok_spec, pl.BlockSpec((nb, 3, d), lambda b, t: (b, 0, 0))]
    if s0 is not None:
        args.append(s0)
        in_specs.append(st_spec)
    args += params
    in_specs += [full(a) for a in params]
    stacks = [state_stack] + ([v_stack] if emit_v else [])
    aliases = {len(args) + i: 1 + i for i in range(len(stacks))}
    args += stacks
    in_specs += [pl.BlockSpec(memory_space=pl.ANY)] * len(stacks)
    out_shape = [jax.ShapeDtypeStruct(x.shape, F32)] + [jax.ShapeDtypeStruct(a.shape, F32) for a in stacks]
    out_specs = [tok_spec, st_spec] + ([v_spec] if emit_v else [])
    return pl.pallas_call(
        functools.partial(_layer_body, cfg),
        grid=(n_b, n_t),
        in_specs=in_specs,
        out_specs=out_specs,
        out_shape=out_shape,
        input_output_aliases=aliases,
        scratch_shapes=[pltpu.VMEM((nb, HGRN_HEADS, HGRN_DK, HGRN_DV), F32),
                        pltpu.VMEM((rows, d), BF16),
                        pltpu.VMEM((rows // hc, d // LANES, hc, LANES), F32),
                        pltpu.VMEM((rows // hc, hc, d), F32)],
        compiler_params=pltpu.CompilerParams(
            dimension_semantics=("arbitrary", "arbitrary"),
            vmem_limit_bytes=V7X_VMEM_LIMIT_BYTES),
        name=f"trunk_layer{layer}_{'sample' if emit_v else 'prompt'}",
    )(*args)


def _split_bf16(a):
    hi = a.astype(BF16)
    return hi, (a - hi.astype(F32)).astype(BF16)


def _ada_body(c_ref, w_ref, b_ref, o_ref):
    n = c_ref.shape[0]
    s_hi, s_lo = _split_bf16(_silu(c_ref[...]))
    w_hi, w_lo = _split_bf16(w_ref[0])
    both = jnp.dot(jnp.concatenate([s_hi, s_lo], axis=0), w_hi, preferred_element_type=F32)
    o_ref[0] = both[:n] + both[n:] + jnp.dot(s_hi, w_lo, preferred_element_type=F32) + b_ref[0]


def _ada_modulation(c_all, w_ada, b_ada):
    depth, d, d3 = w_ada.shape
    n = c_all.shape[0]
    return pl.pallas_call(
        _ada_body,
        grid=(depth, d3 // d),
        in_specs=[pl.BlockSpec((n, d), lambda l, j: (0, 0)),
                  pl.BlockSpec((1, d, d), lambda l, j: (l, 0, j)),
                  pl.BlockSpec((1, 1, d), lambda l, j: (l, 0, j))],
        out_specs=pl.BlockSpec((1, n, d), lambda l, j: (l, 0, j)),
        out_shape=jax.ShapeDtypeStruct((depth, n, d3), F32),
        compiler_params=pltpu.CompilerParams(dimension_semantics=("arbitrary", "arbitrary")),
        name="ada_modulation",
    )(c_all, w_ada, b_ada.reshape(depth, 1, d3))


def kernel(x_prompt, x_sample, state_hgrn, c_prompt, c_sample, w_ada, b_ada, norm_g, w_in, ln_v_g, ln_v_b,
           w_s, b_s, lb_raw, gnorm_g, w_pa, w_pb, w_o, final_g):
    depth, d = norm_g.shape
    n_prompt = x_prompt.shape[0]
    mod = _ada_modulation(jnp.concatenate([c_prompt, c_sample], axis=0), w_ada, b_ada)
    mod = mod.reshape(depth, -1, 3, d)

    xp, xs = x_prompt, x_sample
    n_sample = x_sample.shape[0]
    sp = jnp.zeros((depth, n_prompt, HGRN_HEADS, HGRN_DK, HGRN_DV), F32)
    ss = jnp.zeros((depth, n_sample, HGRN_HEADS, HGRN_DK, HGRN_DV), F32)
    vs = jnp.zeros((depth,) + x_sample.shape, F32)
    for l in range(depth):
        (w_in_l,) = _pack_rows_bf16([w_in], l)
        w_pa_l, w_pb_l, w_o_l = _pack_rows_bf16([w_pa, w_pb, w_o], l)
        w = dict(norm_g=norm_g[l][None], w_in=w_in_l, ln_v_g=ln_v_g[l][None],
                 ln_v_b=ln_v_b[l][None], w_s=w_s[l], b_s=b_s[l], lb_raw=lb_raw, gnorm_g=gnorm_g[l][None],
                 w_pa=w_pa_l, w_pb=w_pb_l, w_o=w_o_l, final_g=final_g[None])
        final = l == depth - 1
        xp, sp = _trunk_layer(l, xp, mod[l, :n_prompt], None, w, final, sp, None)
        xs, ss, vs = _trunk_layer(l, xs, mod[l, n_prompt:], state_hgrn, w, final, ss, vs)
    return (xp, xs, sp, ss, vs)
```
